```python
import math
import jax, jax.numpy as jnp
from jax import lax
import numpy as np

D_MODEL = 1024
BATCH = 32
SEQ = 2048
DEPTH = 1

CHUNK = 64
MIX_WIDTH = D_MODEL
RET_HEADS = 4
RET_VWIDTH = MIX_WIDTH // 2
RET_DV = RET_VWIDTH // RET_HEADS
RET_DK = RET_DV // 2
RET_QKWIDTH = RET_HEADS * RET_DK
POOL_WIDTH = MIX_WIDTH - RET_VWIDTH
POOL_WINDOWS = (2, 4, 8, 16)
POOL_GROUPS = len(POOL_WINDOWS)
POOL_GC = POOL_WIDTH // POOL_GROUPS
IN_WIDTH = 2 * RET_QKWIDTH + 2 * RET_VWIDTH + POOL_WIDTH
D_FF = 2816
ROPE_BASE = 10000.0
RMS_EPS = 1e-6
GN_EPS = 1e-5

kernel_name = "hybrid_retention_multiscale_pool_macaron"


def rms_norm(x, g):
    xf = x.astype(jnp.float32)
    y = xf * lax.rsqrt(jnp.mean(xf * xf, axis=-1, keepdims=True) + RMS_EPS)
    return (y * g.astype(jnp.float32)).astype(x.dtype)


def swiglu(x, w_gate, w_up, w_down):
    return (jax.nn.silu(x @ w_gate) * (x @ w_up)) @ w_down


def rotary(x, pos):
    d = x.shape[-1]
    half = d // 2
    freqs = ROPE_BASE ** (-jnp.arange(half, dtype=jnp.float32) * 2.0 / d)
    ang = pos.astype(jnp.float32)[:, None] * freqs[None, :]
    cos = jnp.cos(ang)[None, :, None, :].astype(x.dtype)
    sin = jnp.sin(ang)[None, :, None, :].astype(x.dtype)
    x1, x2 = x[..., :half], x[..., half:]
    return jnp.concatenate([x1 * cos - x2 * sin, x1 * sin + x2 * cos], axis=-1)


def retention_chunkwise(q, k, v):
    b, s, h, dk = q.shape
    dv = v.shape[-1]
    n = s // CHUNK
    dt = q.dtype
    gamma = 1.0 - 2.0 ** (-5.0 - jnp.arange(h, dtype=jnp.float32))
    log_g = jnp.log(gamma)
    idx = jnp.arange(CHUNK, dtype=jnp.float32)
    d_intra = jnp.exp(log_g[:, None, None] * jnp.abs(idx[:, None] - idx[None, :])).astype(dt)
    d_key = jnp.exp(log_g[:, None] * (CHUNK - 1.0 - idx)[None, :]).astype(dt)
    d_query = jnp.exp(log_g[:, None] * (idx + 1.0)[None, :]).astype(dt)
    d_chunk = jnp.exp(log_g * CHUNK).astype(dt)

    qc = (q * (dk ** -0.5)).reshape(b, n, CHUNK, h, dk)
    kc = k.reshape(b, n, CHUNK, h, dk)
    vc = v.reshape(b, n, CHUNK, h, dv)

    scores = jnp.einsum('bnihd,bnjhd->bnhij', qc, kc) * d_intra[None, None]
    intra = jnp.einsum('bnhij,bnjhe->bnihe', scores, vc)

    kv = jnp.einsum('bnjhd,hj,bnjhe->nbhde', kc, d_key, vc)

    def step(state, kv_n):
        return state * d_chunk[None, :, None, None] + kv_n, state

    init = jnp.zeros((b, h, dk, dv), dtype=kv.dtype)
    _, s_prev = lax.scan(step, init, kv)
    cross = jnp.einsum('bnihd,nbhde,hi->bnihe', qc, s_prev, d_query)
    return (intra + cross).reshape(b, s, h, dv)


def head_group_norm(o, gain):
    b, s, h, dv = o.shape
    of = o.astype(jnp.float32)
    mu = jnp.mean(of, axis=-1, keepdims=True)
    var = jnp.mean(jnp.square(of - mu), axis=-1, keepdims=True)
    y = ((of - mu) * lax.rsqrt(var + GN_EPS)).reshape(b, s, h * dv)
    return (y * gain.astype(jnp.float32)).astype(o.dtype)


def multiscale_pool(u, w_pool, scale):
    b, s, _ = u.shape
    ug = u.reshape(b, s, POOL_GROUPS, POOL_GC)
    cs = jnp.cumsum(ug.astype(jnp.float32), axis=1)
    cs = jnp.concatenate([jnp.zeros((b, 1, POOL_GROUPS, POOL_GC), jnp.float32), cs], axis=1)
    t = jnp.arange(s)
    win = jnp.array(POOL_WINDOWS, dtype=jnp.int32)
    lo = jnp.maximum(t[:, None] + 1 - win[None, :], 0)
    cnt = (t[:, None] + 1 - lo).astype(jnp.float32)
    cs_lo = cs[:, lo, jnp.arange(POOL_GROUPS)[None, :], :]
    mean = (cs[:, 1:] - cs_lo) / cnt[None, :, :, None]
    pooled = mean.astype(u.dtype) - ug
    y = jnp.einsum('bsgc,gcd->bsgd', pooled, w_pool).reshape(b, s, POOL_WIDTH)
    return y * scale


def hybrid_mixer(h, w_in, w_out, ret_gn_gain, pool_w, pool_scale):
    b, s, _ = h.shape
    p = h @ w_in
    o1 = RET_QKWIDTH
    o2 = o1 + RET_QKWIDTH
    o3 = o2 + RET_VWIDTH
    o4 = o3 + RET_VWIDTH
    q = p[..., :o1].reshape(b, s, RET_HEADS, RET_DK)
    k = p[..., o1:o2].reshape(b, s, RET_HEADS, RET_DK)
    v = p[..., o2:o3].reshape(b, s, RET_HEADS, RET_DV)
    g = p[..., o3:o4]
    u = p[..., o4:]
    pos = jnp.arange(s)
    q = rotary(q, pos)
    k = rotary(k, pos)
    ret = retention_chunkwise(q, k, v)
    ret = jax.nn.silu(g) * head_group_norm(ret, ret_gn_gain)
    pool = multiscale_pool(u, pool_w, pool_scale)
    return jnp.concatenate([ret, pool], axis=-1) @ w_out


def setup_inputs(seed: int = 0) -> dict:
    key = jax.random.key(seed)
    ks = jax.random.split(key, 20)
    f32 = jnp.float32
    nrm = lambda k_, shape, fan: jax.random.normal(k_, shape, f32) * (fan ** -0.5)
    gain = lambda k_, shape: 1.0 + 0.02 * jax.random.normal(k_, shape, f32)
    L = DEPTH
    return {
        "x": jax.random.normal(ks[0], (BATCH, SEQ, D_MODEL), f32),
        "norm_ffn1": gain(ks[1], (L, D_MODEL)),
        "ffn1_gate": nrm(ks[2], (L, D_MODEL, D_FF), D_MODEL),
        "ffn1_up": nrm(ks[3], (L, D_MODEL, D_FF), D_MODEL),
        "ffn1_down": nrm(ks[4], (L, D_FF, D_MODEL), D_FF),
        "norm_mix": gain(ks[5], (L, D_MODEL)),
        "w_in": nrm(ks[6], (L, D_MODEL, IN_WIDTH), D_MODEL),
        "ret_gn_gain": gain(ks[7], (L, RET_VWIDTH)),
        "pool_w": nrm(ks[8], (L, POOL_GROUPS, POOL_GC, POOL_GC), POOL_GC),
        "pool_scale": gain(ks[9], (L, POOL_WIDTH)),
        "w_out": nrm(ks[10], (L, MIX_WIDTH, D_MODEL), MIX_WIDTH),
        "norm_ffn2": gain(ks[11], (L, D_MODEL)),
        "ffn2_gate": nrm(ks[12], (L, D_MODEL, D_FF), D_MODEL),
        "ffn2_up": nrm(ks[13], (L, D_MODEL, D_FF), D_MODEL),
        "ffn2_down": nrm(ks[14], (L, D_FF, D_MODEL), D_FF),
        "norm_final": gain(ks[15], (D_MODEL,)),
    }


def reference(x, norm_ffn1, ffn1_gate, ffn1_up, ffn1_down, norm_mix, w_in,
              ret_gn_gain, pool_w, pool_scale, w_out, norm_ffn2, ffn2_gate,
              ffn2_up, ffn2_down, norm_final):
    for l in range(DEPTH):
        x = x + 0.5 * swiglu(rms_norm(x, norm_ffn1[l]), ffn1_gate[l], ffn1_up[l], ffn1_down[l])
        x = x + hybrid_mixer(rms_norm(x, norm_mix[l]), w_in[l], w_out[l],
                             ret_gn_gain[l], pool_w[l], pool_scale[l])
        x = x + 0.5 * swiglu(rms_norm(x, norm_ffn2[l]), ffn2_gate[l], ffn2_up[l], ffn2_down[l])
    return rms_norm(x, norm_final)
```

```python
import functools

import numpy as np
import jax
import jax.numpy as jnp
from jax import lax
from jax.experimental import pallas as pl
from jax.experimental.pallas import tpu as pltpu

D_MODEL = 1024
D_FF = 2816
CHUNK = 64
RET_HEADS = 4
RET_DK = 64
RET_DV = 128
RET_QKWIDTH = RET_HEADS * RET_DK
RET_VWIDTH = RET_HEADS * RET_DV
POOL_WINDOWS = (2, 4, 8, 16)
POOL_GC = 128
POOL_WIDTH = len(POOL_WINDOWS) * POOL_GC
POOL_HALO = 16
IN_WIDTH = 2 * RET_QKWIDTH + 2 * RET_VWIDTH + POOL_WIDTH
ROPE_BASE = 10000.0
RMS_EPS = 1e-6
GN_EPS = 1e-5

RET_BLOCK = 256
FFN_TILE = 512
MIX_TILE = 256

_F32 = jnp.float32
_BF16 = jnp.bfloat16
_MIB = 1024 * 1024


def _rms_norm(x, gain):
    ms = jnp.mean(x * x, axis=-1, keepdims=True)
    return x * lax.rsqrt(ms + RMS_EPS) * gain


def _dot(a, b):
    return jnp.dot(a, b, preferred_element_type=_F32)


def _ffn_kernel(x_ref, g_ref, wg_ref, wu_ref, wd_ref, *rest, final_norm):
    if final_norm:
        gf_ref, o_ref = rest
    else:
        (o_ref,) = rest
    x = x_ref[...]
    n = _rms_norm(x, g_ref[...]).astype(_BF16)
    gate = _dot(n, wg_ref[...])
    up = _dot(n, wu_ref[...])
    h = (gate * jax.nn.sigmoid(gate) * up).astype(_BF16)
    out = x + 0.5 * _dot(h, wd_ref[...])
    if final_norm:
        out = _rms_norm(out, gf_ref[...])
    o_ref[...] = out


def _resident(shape):
    zeros = (0,) * len(shape)
    return pl.BlockSpec(shape, lambda *_: zeros, pipeline_mode=pl.Buffered(1))


def _ffn_call(x2d, gain, wg, wu, wd, final_gain=None):
    rows = x2d.shape[0]
    tm = FFN_TILE
    final_norm = final_gain is not None
    row_spec = pl.BlockSpec((tm, D_MODEL), lambda i: (i, 0))
    in_specs = [row_spec, _resident((1, D_MODEL)), _resident((D_MODEL, D_FF)),
                _resident((D_MODEL, D_FF)), _resident((D_FF, D_MODEL))]
    args = [x2d, gain.reshape(1, D_MODEL), wg, wu, wd]
    if final_norm:
        in_specs.append(_resident((1, D_MODEL)))
        args.append(final_gain.reshape(1, D_MODEL))
    weight_bytes = 3 * D_MODEL * D_FF * 2
    io_bytes = 4 * tm * D_MODEL * 4
    temp_bytes = tm * D_FF * (4 + 4 + 4 + 2) + tm * D_MODEL * (2 + 4 + 4)
    return pl.pallas_call(
        functools.partial(_ffn_kernel, final_norm=final_norm),
        out_shape=jax.ShapeDtypeStruct((rows, D_MODEL), _F32),
        grid=(rows // tm,),
        in_specs=in_specs,
        out_specs=row_spec,
        compiler_params=pltpu.CompilerParams(
            dimension_semantics=("arbitrary",),
            vmem_limit_bytes=weight_bytes + io_bytes + temp_bytes + 4 * _MIB),
        name="ffn_final" if final_norm else "ffn",
    )(*args)


def _mixer_tables(seq):
    heads = np.arange(RET_HEADS)
    gamma = 1.0 - 2.0 ** (-5.0 - heads)
    i = np.arange(RET_BLOCK)
    diff = i[:, None] - i[None, :]
    same = (i[:, None] // CHUNK) == (i[None, :] // CHUNK)
    earlier = (i[None, :] // CHUNK) < (i[:, None] // CHUNK)
    expo = np.where(same, np.abs(diff), diff)
    q_scale = RET_DK ** -0.5
    dmat = np.where((same | earlier)[None], gamma[:, None, None] ** expo[None], 0.0) * q_scale
    dq = np.repeat((gamma[None, :] ** (i[:, None] + 1.0)) * q_scale, RET_DV, axis=1)
    dk = np.repeat(gamma[None, :] ** (RET_BLOCK - 1.0 - i[:, None]), RET_DK, axis=1)
    dc = np.repeat(gamma ** float(RET_BLOCK), RET_DV)[None, :]
    bm = (np.arange(RET_QKWIDTH)[:, None] // RET_DK ==
          np.arange(RET_VWIDTH)[None, :] // RET_DV).astype(np.float64)
    half = RET_DK // 2
    freqs = ROPE_BASE ** (-np.arange(half) * 2.0 / RET_DK)
    ang = np.arange(seq)[:, None] * freqs[None, :]
    cos = np.tile(np.cos(ang), (1, 4))
    sin = np.tile(np.concatenate([-np.sin(ang), np.sin(ang)], axis=1), (1, 2))
    f32 = lambda a: jnp.asarray(a, dtype=_F32)
    return tuple(f32(a) for a in (cos, sin, dmat, dq, dk, dc, bm))


def _rotary(t, cos, sin):
    lane = lax.broadcasted_iota(jnp.int32, (t.shape[0], 128), 1)
    first_half = (lane & (RET_DK - 1)) < (RET_DK // 2)
    cols = []
    for c in range(t.shape[1] // 128):
        tc = t[:, c * 128:(c + 1) * 128]
        partner = jnp.where(first_half, pltpu.roll(tc, 128 - RET_DK // 2, 1),
                            pltpu.roll(tc, RET_DK // 2, 1))
        cols.append(tc * cos + partner * sin)
    return jnp.concatenate(cols, axis=1)


def _mixer_kernel(x_ref, g_ref, win_ref, wout_ref, gain_ref, pw_ref, ps_ref,
                  cos_ref, sin_ref, dmat_ref, dq_ref, dk_ref, dc_ref, bm_ref,
                  o_ref, state_ref, u_ref, *, tm):
    s = pl.program_id(1)
    o1, o2, o3, o4 = RET_QKWIDTH, 2 * RET_QKWIDTH, 2 * RET_QKWIDTH + RET_VWIDTH, IN_WIDTH - POOL_WIDTH

    @pl.when(s == 0)
    def _():
        state_ref[...] = jnp.zeros_like(state_ref)
        u_ref[0:POOL_HALO, :] = jnp.zeros((POOL_HALO, POOL_WIDTH), _F32)

    @pl.when(s > 0)
    def _():
        u_ref[0:POOL_HALO, :] = u_ref[tm:tm + POOL_HALO, :]

    x = x_ref[...]
    n = _rms_norm(x, g_ref[...]).astype(_BF16)
    p = _dot(n, win_ref[...])

    row0 = pl.multiple_of(s * tm, tm)
    cos = cos_ref[pl.ds(row0, tm), :]
    sin = sin_ref[pl.ds(row0, tm), :]
    q = _rotary(p[:, :o1], cos, sin).astype(_BF16)
    k = _rotary(p[:, o1:o2], cos, sin)
    v = p[:, o2:o3].astype(_BF16)

    lane_head = lax.broadcasted_iota(jnp.int32, (RET_BLOCK, RET_QKWIDTH), 1) // RET_DK
    ret_blocks = []
    for b in range(tm // RET_BLOCK):
        rows = slice(b * RET_BLOCK, (b + 1) * RET_BLOCK)
        qb, kb, vb = q[rows], k[rows], v[rows]
        state = state_ref[...]
        cross = _dot(qb, state.astype(_BF16)) * dq_ref[...]
        intra = []
        for h in range(RET_HEADS):
            kh = jnp.where(lane_head == h, kb, 0.0).astype(_BF16)
            sc = lax.dot_general(qb, kh, (((1,), (1,)), ((), ())),
                                 preferred_element_type=_F32)
            a = (sc * dmat_ref[h]).astype(_BF16)
            intra.append(_dot(a, vb[:, h * RET_DV:(h + 1) * RET_DV]))
        ret_blocks.append(jnp.concatenate(intra, axis=1) + cross)
        kd = (kb * dk_ref[...]).astype(_BF16)
        kv = lax.dot_general(kd, vb, (((0,), (0,)), ((), ())),
                             preferred_element_type=_F32)
        state_ref[...] = state * dc_ref[...] + kv * bm_ref[...]
    ret = ret_blocks[0] if len(ret_blocks) == 1 else jnp.concatenate(ret_blocks, axis=0)

    gn = []
    for h in range(RET_HEADS):
        o_h = ret[:, h * RET_DV:(h + 1) * RET_DV]
        mu = jnp.mean(o_h, axis=-1, keepdims=True)
        d = o_h - mu
        var = jnp.mean(d * d, axis=-1, keepdims=True)
        gn.append(d * lax.rsqrt(var + GN_EPS))
    gate = p[:, o3:o4]
    r = (gate * jax.nn.sigmoid(gate)) * (jnp.concatenate(gn, axis=1) * gain_ref[...])

    u = p[:, o4:]
    u_ref[POOL_HALO:POOL_HALO + tm, :] = u
    pos = row0 + lax.broadcasted_iota(jnp.int32, (tm, 1), 0)
    pooled = []
    for gi, w in enumerate(POOL_WINDOWS):
        cs = slice(gi * POOL_GC, (gi + 1) * POOL_GC)
        acc = u[:, cs]
        for d in range(1, w):
            acc = acc + u_ref[POOL_HALO - d:POOL_HALO - d + tm, cs]
        cnt = jnp.minimum(pos + 1, w).astype(_F32)
        pooled_g = (acc / cnt - u[:, cs]).astype(_BF16)
        pooled.append(_dot(pooled_g, pw_ref[gi]))
    pool = jnp.concatenate(pooled, axis=1) * ps_ref[...]

    mix = (_dot(r.astype(_BF16), wout_ref[0:RET_VWIDTH, :]) +
           _dot(pool.astype(_BF16), wout_ref[RET_VWIDTH:, :]))
    o_ref[...] = x + mix


def _mixer_call(x2d, batch, seq, gain, w_in, w_out, gn_gain, pool_w, pool_scale):
    tm = MIX_TILE
    steps = seq // tm
    tables = _mixer_tables(seq)
    row_spec = pl.BlockSpec((tm, D_MODEL), lambda b, s: (b * steps + s, 0))
    small = [gain.reshape(1, D_MODEL), w_in, w_out, gn_gain.reshape(1, RET_VWIDTH),
             pool_w, pool_scale.reshape(1, POOL_WIDTH), *tables]
    in_specs = [row_spec] + [_resident(a.shape) for a in small]
    resident_bytes = sum(int(np.prod(a.shape)) * a.dtype.itemsize for a in small)
    temp_bytes = tm * (IN_WIDTH * 4 * 2 + D_MODEL * 4 * 6) + 4 * RET_BLOCK * RET_BLOCK * 8
    return pl.pallas_call(
        functools.partial(_mixer_kernel, tm=tm),
        out_shape=jax.ShapeDtypeStruct(x2d.shape, _F32),
        grid=(batch, steps),
        in_specs=in_specs,
        out_specs=row_spec,
        scratch_shapes=[pltpu.VMEM((RET_QKWIDTH, RET_VWIDTH), _F32),
                        pltpu.VMEM((POOL_HALO + tm, POOL_WIDTH), _F32)],
        compiler_params=pltpu.CompilerParams(
            dimension_semantics=("arbitrary", "arbitrary"),
            vmem_limit_bytes=resident_bytes + 4 * tm * D_MODEL * 4 + temp_bytes + 8 * _MIB),
        name="mixer",
    )(x2d, *small)


def kernel(x, norm_ffn1, ffn1_gate, ffn1_up, ffn1_down, norm_mix, w_in, ret_gn_gain, pool_w,
           pool_scale, w_out, norm_ffn2, ffn2_gate, ffn2_up, ffn2_down, norm_final):
    batch, seq, d_model = x.shape
    assert d_model == D_MODEL and seq % MIX_TILE == 0 and (batch * seq) % FFN_TILE == 0
    bf = lambda a: a.astype(_BF16)
    h = x.reshape(batch * seq, d_model)
    depth = norm_ffn1.shape[0]
    for l in range(depth):
        last = l == depth - 1
        h = _ffn_call(h, norm_ffn1[l], bf(ffn1_gate[l]), bf(ffn1_up[l]), bf(ffn1_down[l]))
        h = _mixer_call(h, batch, seq, norm_mix[l], bf(w_in[l]), bf(w_out[l]), ret_gn_gain[l],
                        bf(pool_w[l]), pool_scale[l])
        h = _ffn_call(h, norm_ffn2[l], bf(ffn2_gate[l]), bf(ffn2_up[l]), bf(ffn2_down[l]),
                      final_gain=norm_final if last else None)
    return h.reshape(batch, seq, d_model)
```

```python
import functools

import numpy as np
import jax
import jax.numpy as jnp
from jax import lax
from jax.experimental import pallas as pl
from jax.experimental.pallas import tpu as pltpu

D_MODEL = 1024
D_FF = 2816
CHUNK = 64
RET_HEADS = 4
RET_DK = 64
RET_DV = 128
RET_QKWIDTH = RET_HEADS * RET_DK
RET_VWIDTH = RET_HEADS * RET_DV
POOL_WINDOWS = (2, 4, 8, 16)
POOL_GC = 128
POOL_WIDTH = len(POOL_WINDOWS) * POOL_GC
POOL_HALO = 16
IN_WIDTH = 2 * RET_QKWIDTH + 2 * RET_VWIDTH + POOL_WIDTH
ROPE_BASE = 10000.0
RMS_EPS = 1e-6
GN_EPS = 1e-5

RET_BLOCK = 256
FFN_TILE = 512
MIX_TILE = 1024

_F32 = jnp.float32
_BF16 = jnp.bfloat16
_MIB = 1024 * 1024


def _rms_norm(x, gain):
    ms = jnp.mean(x * x, axis=-1, keepdims=True)
    return x * lax.rsqrt(ms + RMS_EPS) * gain


def _dot(a, b):
    return jnp.dot(a, b, preferred_element_type=_F32)


def _ffn_kernel(x_ref, g_ref, wg_ref, wu_ref, wd_ref, *rest, final_norm):
    if final_norm:
        gf_ref, o_ref = rest
    else:
        (o_ref,) = rest
    x = x_ref[...]
    n = _rms_norm(x, g_ref[...]).astype(_BF16)
    gate = _dot(n, wg_ref[...])
    up = _dot(n, wu_ref[...])
    h = (gate * jax.nn.sigmoid(gate) * up).astype(_BF16)
    out = x + 0.5 * _dot(h, wd_ref[...])
    if final_norm:
        out = _rms_norm(out, gf_ref[...])
    o_ref[...] = out


def _resident(shape):
    zeros = (0,) * len(shape)
    return pl.BlockSpec(shape, lambda *_: zeros, pipeline_mode=pl.Buffered(1))


def _ffn_call(x2d, gain, wg, wu, wd, final_gain=None):
    rows = x2d.shape[0]
    tm = FFN_TILE
    final_norm = final_gain is not None
    row_spec = pl.BlockSpec((tm, D_MODEL), lambda i: (i, 0))
    in_specs = [row_spec, _resident((1, D_MODEL)), _resident((D_MODEL, D_FF)),
                _resident((D_MODEL, D_FF)), _resident((D_FF, D_MODEL))]
    args = [x2d, gain.reshape(1, D_MODEL), wg, wu, wd]
    if final_norm:
        in_specs.append(_resident((1, D_MODEL)))
        args.append(final_gain.reshape(1, D_MODEL))
    weight_bytes = 3 * D_MODEL * D_FF * 2
    io_bytes = 4 * tm * D_MODEL * 4
    temp_bytes = tm * D_FF * (4 + 4 + 4 + 2) + tm * D_MODEL * (2 + 4 + 4)
    return pl.pallas_call(
        functools.partial(_ffn_kernel, final_norm=final_norm),
        out_shape=jax.ShapeDtypeStruct((rows, D_MODEL), _F32),
        grid=(rows // tm,),
        in_specs=in_specs,
        out_specs=row_spec,
        compiler_params=pltpu.CompilerParams(
            dimension_semantics=("arbitrary",),
            vmem_limit_bytes=weight_bytes + io_bytes + temp_bytes + 4 * _MIB),
        name="ffn_final" if final_norm else "ffn",
    )(*args)


def _mixer_tables(seq):
    heads = np.arange(RET_HEADS)
    gamma = 1.0 - 2.0 ** (-5.0 - heads)
    i = np.arange(RET_BLOCK)
    diff = i[:, None] - i[None, :]
    same = (i[:, None] // CHUNK) == (i[None, :] // CHUNK)
    earlier = (i[None, :] // CHUNK) < (i[:, None] // CHUNK)
    expo = np.where(same, np.abs(diff), diff)
    q_scale = RET_DK ** -0.5
    dmat = np.where((same | earlier)[None], gamma[:, None, None] ** expo[None], 0.0) * q_scale
    dq = np.repeat((gamma[None, :] ** (i[:, None] + 1.0)) * q_scale, RET_DV, axis=1)
    dk = np.repeat(gamma[None, :] ** (RET_BLOCK - 1.0 - i[:, None]), RET_DK, axis=1)
    dc = np.repeat(gamma ** float(RET_BLOCK), RET_DV)[None, :]
    bm = (np.arange(RET_QKWIDTH)[:, None] // RET_DK ==
          np.arange(RET_VWIDTH)[None, :] // RET_DV).astype(np.float64)
    half = RET_DK // 2
    freqs = ROPE_BASE ** (-np.arange(half) * 2.0 / RET_DK)
    ang = np.arange(seq)[:, None] * freqs[None, :]
    cos = np.tile(np.cos(ang), (1, 4))
    sin = np.tile(np.concatenate([-np.sin(ang), np.sin(ang)], axis=1), (1, 2))
    f32 = lambda a: jnp.asarray(a, dtype=_F32)
    return tuple(f32(a) for a in (cos, sin, dmat, dq, dk, dc, bm))


def _rotary(t, cos, sin):
    lane = lax.broadcasted_iota(jnp.int32, (t.shape[0], 128), 1)
    first_half = (lane & (RET_DK - 1)) < (RET_DK // 2)
    cols = []
    for c in range(t.shape[1] // 128):
        tc = t[:, c * 128:(c + 1) * 128]
        partner = jnp.where(first_half, pltpu.roll(tc, 128 - RET_DK // 2, 1),
                            pltpu.roll(tc, RET_DK // 2, 1))
        cols.append(tc * cos + partner * sin)
    return jnp.concatenate(cols, axis=1)


def _mixer_kernel(x_ref, g_ref, win_ref, wout_ref, gain_ref, pw_ref, ps_ref,
                  cos_ref, sin_ref, dmat_ref, dq_ref, dk_ref, dc_ref, bm_ref,
                  o_ref, state_ref, u_ref, *, tm):
    s = pl.program_id(1)
    o1, o2, o3, o4 = RET_QKWIDTH, 2 * RET_QKWIDTH, 2 * RET_QKWIDTH + RET_VWIDTH, IN_WIDTH - POOL_WIDTH

    @pl.when(s == 0)
    def _():
        state_ref[...] = jnp.zeros_like(state_ref)
        u_ref[0:POOL_HALO, :] = jnp.zeros((POOL_HALO, POOL_WIDTH), _F32)

    @pl.when(s > 0)
    def _():
        u_ref[0:POOL_HALO, :] = u_ref[tm:tm + POOL_HALO, :]

    x = x_ref[...]
    n = _rms_norm(x, g_ref[...]).astype(_BF16)
    row0 = pl.multiple_of(s * tm, tm)

    u = _dot(n, win_ref[:, o4:])
    u_ref[POOL_HALO:POOL_HALO + tm, :] = u
    pos = row0 + lax.broadcasted_iota(jnp.int32, (tm, 1), 0)
    pooled = []
    for gi, w in enumerate(POOL_WINDOWS):
        cs = slice(gi * POOL_GC, (gi + 1) * POOL_GC)
        acc = u_ref[:, cs]
        shift = 1
        while shift < w:
            acc = acc + pltpu.roll(acc, shift, 0)
            shift *= 2
        cnt = jnp.minimum(pos + 1, w).astype(_F32)
        pooled.append((acc[POOL_HALO:, :] / cnt - u[:, cs]).astype(_BF16))

    gate = _dot(n, win_ref[:, o3:o4])
    gate = gate * jax.nn.sigmoid(gate)
    qkv = _dot(n, win_ref[:, :o3])
    pool = jnp.concatenate([_dot(pooled[gi], pw_ref[gi]) for gi in range(len(POOL_WINDOWS))],
                           axis=1)
    pool = (pool * ps_ref[...]).astype(_BF16)

    cos = cos_ref[pl.ds(row0, tm), :]
    sin = sin_ref[pl.ds(row0, tm), :]
    q = _rotary(qkv[:, :o1], cos, sin).astype(_BF16)
    k = _rotary(qkv[:, o1:o2], cos, sin)
    v = qkv[:, o2:o3].astype(_BF16)

    lane_head = lax.broadcasted_iota(jnp.int32, (RET_BLOCK, RET_QKWIDTH), 1) // RET_DK
    blocks = [slice(b * RET_BLOCK, (b + 1) * RET_BLOCK) for b in range(tm // RET_BLOCK)]
    states = [state_ref[...]]
    intra_blocks = []
    for rows in blocks:
        qb, kb, vb = q[rows], k[rows], v[rows]
        decayed = []
        for h in range(RET_HEADS):
            kh = jnp.where(lane_head == h, kb, 0.0).astype(_BF16)
            sc = lax.dot_general(qb, kh, (((1,), (1,)), ((), ())),
                                 preferred_element_type=_F32)
            decayed.append((sc * dmat_ref[h]).astype(_BF16))
        kd = (kb * dk_ref[...]).astype(_BF16)
        kv = lax.dot_general(kd, vb, (((0,), (0,)), ((), ())),
                             preferred_element_type=_F32)
        states.append(states[-1] * dc_ref[...] + kv * bm_ref[...])
        intra_blocks.append(jnp.concatenate(
            [_dot(decayed[h], vb[:, h * RET_DV:(h + 1) * RET_DV]) for h in range(RET_HEADS)],
            axis=1))
    state_ref[...] = states[-1]
    ret_blocks = []
    for rows, state, intra in zip(blocks, states, intra_blocks):
        cross = _dot(q[rows], state.astype(_BF16)) * dq_ref[...]
        ret_blocks.append(intra + cross)
    ret = ret_blocks[0] if len(ret_blocks) == 1 else jnp.concatenate(ret_blocks, axis=0)

    gn = []
    for h in range(RET_HEADS):
        o_h = ret[:, h * RET_DV:(h + 1) * RET_DV]
        mu = jnp.mean(o_h, axis=-1, keepdims=True)
        d = o_h - mu
        var = jnp.mean(d * d, axis=-1, keepdims=True)
        gn.append(d * lax.rsqrt(var + GN_EPS))
    r = gate * (jnp.concatenate(gn, axis=1) * gain_ref[...])

    mix = (_dot(r.astype(_BF16), wout_ref[0:RET_VWIDTH, :]) +
           _dot(pool, wout_ref[RET_VWIDTH:, :]))
    o_ref[...] = x + mix


def _mixer_call(x2d, batch, seq, gain, w_in, w_out, gn_gain, pool_w, pool_scale):
    tm = MIX_TILE
    steps = seq // tm
    tables = _mixer_tables(seq)
    row_spec = pl.BlockSpec((tm, D_MODEL), lambda b, s: (b * steps + s, 0))
    small = [gain.reshape(1, D_MODEL), w_in, w_out, gn_gain.reshape(1, RET_VWIDTH),
             pool_w, pool_scale.reshape(1, POOL_WIDTH), *tables]
    in_specs = [row_spec] + [_resident(a.shape) for a in small]
    resident_bytes = sum(int(np.prod(a.shape)) * a.dtype.itemsize for a in small)
    temp_bytes = tm * (IN_WIDTH * 4 * 2 + D_MODEL * 4 * 6) + 4 * RET_BLOCK * RET_BLOCK * 8
    return pl.pallas_call(
        functools.partial(_mixer_kernel, tm=tm),
        out_shape=jax.ShapeDtypeStruct(x2d.shape, _F32),
        grid=(batch, steps),
        in_specs=in_specs,
        out_specs=row_spec,
        scratch_shapes=[pltpu.VMEM((RET_QKWIDTH, RET_VWIDTH), _F32),
                        pltpu.VMEM((POOL_HALO + tm, POOL_WIDTH), _F32)],
        compiler_params=pltpu.CompilerParams(
            dimension_semantics=("arbitrary", "arbitrary"),
            vmem_limit_bytes=resident_bytes + 4 * tm * D_MODEL * 4 + temp_bytes + 8 * _MIB),
        name="mixer",
    )(x2d, *small)


def kernel(x, norm_ffn1, ffn1_gate, ffn1_up, ffn1_down, norm_mix, w_in, ret_gn_gain, pool_w,
           pool_scale, w_out, norm_ffn2, ffn2_gate, ffn2_up, ffn2_down, norm_final):
    batch, seq, d_model = x.shape
    assert d_model == D_MODEL and seq % MIX_TILE == 0 and (batch * seq) % FFN_TILE == 0
    bf = lambda a: a.astype(_BF16)
    h = x.reshape(batch * seq, d_model)
    depth = norm_ffn1.shape[0]
    for l in range(depth):
        last = l == depth - 1
        h = _ffn_call(h, norm_ffn1[l], bf(ffn1_gate[l]), bf(ffn1_up[l]), bf(ffn1_down[l]))
        h = _mixer_call(h, batch, seq, norm_mix[l], bf(w_in[l]), bf(w_out[l]), ret_gn_gain[l],
                        bf(pool_w[l]), pool_scale[l])
        h = _ffn_call(h, norm_ffn2[l], bf(ffn2_gate[l]), bf(ffn2_up[l]), bf(ffn2_down[l]),
                      final_gain=norm_final if last else None)
    return h.reshape(batch, seq, d_model)
```

```python
import functools

import numpy as np
import jax
import jax.numpy as jnp
from jax import lax
from jax.experimental import pallas as pl
from jax.experimental.pallas import tpu as pltpu

D_MODEL = 1024
D_FF = 2816
CHUNK = 64
RET_HEADS = 4
RET_DK = 64
RET_DV = 128
RET_QKWIDTH = RET_HEADS * RET_DK
RET_VWIDTH = RET_HEADS * RET_DV
POOL_WINDOWS = (2, 4, 8, 16)
POOL_GC = 128
POOL_WIDTH = len(POOL_WINDOWS) * POOL_GC
POOL_HALO = 16
IN_WIDTH = 2 * RET_QKWIDTH + 2 * RET_VWIDTH + POOL_WIDTH
ROPE_BASE = 10000.0
RMS_EPS = 1e-6
GN_EPS = 1e-5

RET_BLOCK = 256
FFN_TILE = 1024
FFN_SUB = 512
FF_CHUNK = 256
MIX_TILE = 1024

_F32 = jnp.float32
_BF16 = jnp.bfloat16
_MIB = 1024 * 1024


def _rms_norm(x, gain):
    ms = jnp.mean(x * x, axis=-1, keepdims=True)
    return x * lax.rsqrt(ms + RMS_EPS) * gain


def _dot(a, b):
    return jnp.dot(a, b, preferred_element_type=_F32)


def _ffn_kernel(x_ref, g_ref, wg_ref, wu_ref, wd_ref, *rest, final_norm):
    if final_norm:
        gf_ref, o_ref = rest
    else:
        (o_ref,) = rest
    for sub in range(x_ref.shape[0] // FFN_SUB):
        rows = slice(sub * FFN_SUB, (sub + 1) * FFN_SUB)
        x = x_ref[rows, :]
        n = _rms_norm(x, g_ref[...]).astype(_BF16)
        hidden = []
        for c in range(D_FF // FF_CHUNK):
            cols = slice(c * FF_CHUNK, (c + 1) * FF_CHUNK)
            gate = _dot(n, wg_ref[:, cols])
            up = _dot(n, wu_ref[:, cols])
            hidden.append((gate * jax.nn.sigmoid(gate) * up).astype(_BF16))
        h = jnp.concatenate(hidden, axis=1)
        out = x + 0.5 * _dot(h, wd_ref[...])
        if final_norm:
            out = _rms_norm(out, gf_ref[...])
        o_ref[rows, :] = out


def _resident(shape):
    zeros = (0,) * len(shape)
    return pl.BlockSpec(shape, lambda *_: zeros, pipeline_mode=pl.Buffered(1))


def _ffn_call(x2d, gain, wg, wu, wd, final_gain=None):
    rows = x2d.shape[0]
    tm = FFN_TILE
    final_norm = final_gain is not None
    row_spec = pl.BlockSpec((tm, D_MODEL), lambda i: (i, 0))
    in_specs = [row_spec, _resident((1, D_MODEL)), _resident((D_MODEL, D_FF)),
                _resident((D_MODEL, D_FF)), _resident((D_FF, D_MODEL))]
    args = [x2d, gain.reshape(1, D_MODEL), wg, wu, wd]
    if final_norm:
        in_specs.append(_resident((1, D_MODEL)))
        args.append(final_gain.reshape(1, D_MODEL))
    weight_bytes = 3 * D_MODEL * D_FF * 2
    io_bytes = 4 * tm * D_MODEL * 4
    temp_bytes = tm * D_FF * 2 * 2 + tm * D_MODEL * (2 + 4 + 4) + 8 * FFN_SUB * FF_CHUNK * 4
    return pl.pallas_call(
        functools.partial(_ffn_kernel, final_norm=final_norm),
        out_shape=jax.ShapeDtypeStruct((rows, D_MODEL), _F32),
        grid=(rows // tm,),
        in_specs=in_specs,
        out_specs=row_spec,
        compiler_params=pltpu.CompilerParams(
            dimension_semantics=("arbitrary",),
            vmem_limit_bytes=weight_bytes + io_bytes + temp_bytes + 4 * _MIB),
        name="ffn_final" if final_norm else "ffn",
    )(*args)


def _mixer_tables(seq):
    heads = np.arange(RET_HEADS)
    gamma = 1.0 - 2.0 ** (-5.0 - heads)
    i = np.arange(RET_BLOCK)
    diff = i[:, None] - i[None, :]
    same = (i[:, None] // CHUNK) == (i[None, :] // CHUNK)
    earlier = (i[None, :] // CHUNK) < (i[:, None] // CHUNK)
    expo = np.where(same, np.abs(diff), diff)
    q_scale = RET_DK ** -0.5
    dmat = np.where((same | earlier)[None], gamma[:, None, None] ** expo[None], 0.0) * q_scale
    dq = np.repeat((gamma[None, :] ** (i[:, None] + 1.0)) * q_scale, RET_DV, axis=1)
    dk = np.repeat(gamma[None, :] ** (RET_BLOCK - 1.0 - i[:, None]), RET_DK, axis=1)
    dc = np.repeat(gamma ** float(RET_BLOCK), RET_DV)[None, :]
    bm = (np.arange(RET_QKWIDTH)[:, None] // RET_DK ==
          np.arange(RET_VWIDTH)[None, :] // RET_DV).astype(np.float64)
    half = RET_DK // 2
    freqs = ROPE_BASE ** (-np.arange(half) * 2.0 / RET_DK)
    ang = np.arange(seq)[:, None] * freqs[None, :]
    cos = np.tile(np.cos(ang), (1, 4))
    sin = np.tile(np.concatenate([-np.sin(ang), np.sin(ang)], axis=1), (1, 2))
    f32 = lambda a: jnp.asarray(a, dtype=_F32)
    return tuple(f32(a) for a in (cos, sin, dmat, dq, dk, dc, bm))


def _rotary(t, cos, sin):
    lane = lax.broadcasted_iota(jnp.int32, (t.shape[0], 128), 1)
    first_half = (lane & (RET_DK - 1)) < (RET_DK // 2)
    cols = []
    for c in range(t.shape[1] // 128):
        tc = t[:, c * 128:(c + 1) * 128]
        partner = jnp.where(first_half, pltpu.roll(tc, 128 - RET_DK // 2, 1),
                            pltpu.roll(tc, RET_DK // 2, 1))
        cols.append(tc * cos + partner * sin)
    return jnp.concatenate(cols, axis=1)


def _mixer_kernel(x_ref, g_ref, win_ref, wout_ref, gain_ref, pw_ref, ps_ref,
                  cos_ref, sin_ref, dmat_ref, dq_ref, dk_ref, dc_ref, bm_ref,
                  o_ref, state_ref, u_ref, *, tm):
    s = pl.program_id(1)
    o1, o2, o3, o4 = RET_QKWIDTH, 2 * RET_QKWIDTH, 2 * RET_QKWIDTH + RET_VWIDTH, IN_WIDTH - POOL_WIDTH

    @pl.when(s == 0)
    def _():
        state_ref[...] = jnp.zeros_like(state_ref)
        u_ref[0:POOL_HALO, :] = jnp.zeros((POOL_HALO, POOL_WIDTH), _F32)

    @pl.when(s > 0)
    def _():
        u_ref[0:POOL_HALO, :] = u_ref[tm:tm + POOL_HALO, :]

    x = x_ref[...]
    n = _rms_norm(x, g_ref[...]).astype(_BF16)
    row0 = pl.multiple_of(s * tm, tm)

    u = _dot(n, win_ref[:, o4:])
    u_ref[POOL_HALO:POOL_HALO + tm, :] = u
    pos = row0 + lax.broadcasted_iota(jnp.int32, (tm, 1), 0)
    pooled = []
    for gi, w in enumerate(POOL_WINDOWS):
        cs = slice(gi * POOL_GC, (gi + 1) * POOL_GC)
        acc = u_ref[:, cs]
        shift = 1
        while shift < w:
            acc = acc + pltpu.roll(acc, shift, 0)
            shift *= 2
        cnt = jnp.minimum(pos + 1, w).astype(_F32)
        pooled.append((acc[POOL_HALO:, :] / cnt - u[:, cs]).astype(_BF16))

    gate = _dot(n, win_ref[:, o3:o4])
    gate = gate * jax.nn.sigmoid(gate)
    qkv = _dot(n, win_ref[:, :o3])
    pool = jnp.concatenate([_dot(pooled[gi], pw_ref[gi]) for gi in range(len(POOL_WINDOWS))],
                           axis=1)
    pool = (pool * ps_ref[...]).astype(_BF16)

    cos = cos_ref[pl.ds(row0, tm), :]
    sin = sin_ref[pl.ds(row0, tm), :]
    q = _rotary(qkv[:, :o1], cos, sin).astype(_BF16)
    k = _rotary(qkv[:, o1:o2], cos, sin)
    v = qkv[:, o2:o3].astype(_BF16)

    lane_head = lax.broadcasted_iota(jnp.int32, (RET_BLOCK, RET_QKWIDTH), 1) // RET_DK
    blocks = [slice(b * RET_BLOCK, (b + 1) * RET_BLOCK) for b in range(tm // RET_BLOCK)]
    states = [state_ref[...]]
    intra_blocks = []
    for rows in blocks:
        qb, kb, vb = q[rows], k[rows], v[rows]
        decayed = []
        for h in range(RET_HEADS):
            kh = jnp.where(lane_head == h, kb, 0.0).astype(_BF16)
            sc = lax.dot_general(qb, kh, (((1,), (1,)), ((), ())),
                                 preferred_element_type=_F32)
            decayed.append((sc * dmat_ref[h]).astype(_BF16))
        kd = (kb * dk_ref[...]).astype(_BF16)
        kv = lax.dot_general(kd, vb, (((0,), (0,)), ((), ())),
                             preferred_element_type=_F32)
        states.append(states[-1] * dc_ref[...] + kv * bm_ref[...])
        intra_blocks.append(jnp.concatenate(
            [_dot(decayed[h], vb[:, h * RET_DV:(h + 1) * RET_DV]) for h in range(RET_HEADS)],
            axis=1))
    state_ref[...] = states[-1]
    ret_blocks = []
    for rows, state, intra in zip(blocks, states, intra_blocks):
        cross = _dot(q[rows], state.astype(_BF16)) * dq_ref[...]
        ret_blocks.append(intra + cross)
    ret = ret_blocks[0] if len(ret_blocks) == 1 else jnp.concatenate(ret_blocks, axis=0)

    gn = []
    for h in range(RET_HEADS):
        o_h = ret[:, h * RET_DV:(h + 1) * RET_DV]
        mu = jnp.mean(o_h, axis=-1, keepdims=True)
        d = o_h - mu
        var = jnp.mean(d * d, axis=-1, keepdims=True)
        gn.append(d * lax.rsqrt(var + GN_EPS))
    r = gate * (jnp.concatenate(gn, axis=1) * gain_ref[...])

    mix = (_dot(r.astype(_BF16), wout_ref[0:RET_VWIDTH, :]) +
           _dot(pool, wout_ref[RET_VWIDTH:, :]))
    o_ref[...] = x + mix


def _mixer_call(x2d, batch, seq, gain, w_in, w_out, gn_gain, pool_w, pool_scale):
    tm = MIX_TILE
    steps = seq // tm
    tables = _mixer_tables(seq)
    row_spec = pl.BlockSpec((tm, D_MODEL), lambda b, s: (b * steps + s, 0))
    small = [gain.reshape(1, D_MODEL), w_in, w_out, gn_gain.reshape(1, RET_VWIDTH),
             pool_w, pool_scale.reshape(1, POOL_WIDTH), *tables]
    in_specs = [row_spec] + [_resident(a.shape) for a in small]
    resident_bytes = sum(int(np.prod(a.shape)) * a.dtype.itemsize for a in small)
    temp_bytes = tm * (IN_WIDTH * 4 * 2 + D_MODEL * 4 * 6) + 4 * RET_BLOCK * RET_BLOCK * 8
    return pl.pallas_call(
        functools.partial(_mixer_kernel, tm=tm),
        out_shape=jax.ShapeDtypeStruct(x2d.shape, _F32),
        grid=(batch, steps),
        in_specs=in_specs,
        out_specs=row_spec,
        scratch_shapes=[pltpu.VMEM((RET_QKWIDTH, RET_VWIDTH), _F32),
                        pltpu.VMEM((POOL_HALO + tm, POOL_WIDTH), _F32)],
        compiler_params=pltpu.CompilerParams(
            dimension_semantics=("arbitrary", "arbitrary"),
            vmem_limit_bytes=resident_bytes + 4 * tm * D_MODEL * 4 + temp_bytes + 8 * _MIB),
        name="mixer",
    )(x2d, *small)


def kernel(x, norm_ffn1, ffn1_gate, ffn1_up, ffn1_down, norm_mix, w_in, ret_gn_gain, pool_w,
           pool_scale, w_out, norm_ffn2, ffn2_gate, ffn2_up, ffn2_down, norm_final):
    batch, seq, d_model = x.shape
    assert d_model == D_MODEL and seq % MIX_TILE == 0 and (batch * seq) % FFN_TILE == 0
    bf = lambda a: a.astype(_BF16)
    h = x.reshape(batch * seq, d_model)
    depth = norm_ffn1.shape[0]
    for l in range(depth):
        last = l == depth - 1
        h = _ffn_call(h, norm_ffn1[l], bf(ffn1_gate[l]), bf(ffn1_up[l]), bf(ffn1_down[l]))
        h = _mixer_call(h, batch, seq, norm_mix[l], bf(w_in[l]), bf(w_out[l]), ret_gn_gain[l],
                        bf(pool_w[l]), pool_scale[l])
        h = _ffn_call(h, norm_ffn2[l], bf(ffn2_gate[l]), bf(ffn2_up[l]), bf(ffn2_down[l]),
                      final_gain=norm_final if last else None)
    return h.reshape(batch, seq, d_model)
```

```python
import functools

import numpy as np
import jax
import jax.numpy as jnp
from jax import lax
from jax.experimental import pallas as pl
from jax.experimental.pallas import tpu as pltpu

D_MODEL = 1024
D_FF = 2816
CHUNK = 64
RET_HEADS = 4
RET_DK = 64
RET_DV = 128
RET_QKWIDTH = RET_HEADS * RET_DK
RET_VWIDTH = RET_HEADS * RET_DV
POOL_WINDOWS = (2, 4, 8, 16)
POOL_GC = 128
POOL_WIDTH = len(POOL_WINDOWS) * POOL_GC
POOL_HALO = 16
IN_WIDTH = 2 * RET_QKWIDTH + 2 * RET_VWIDTH + POOL_WIDTH
ROPE_BASE = 10000.0
RMS_EPS = 1e-6
GN_EPS = 1e-5

RET_BLOCK = 256
FFN_TILE = 1024
FFN_SUB = 512
FF_CHUNK = 256
MIX_TILE = 1024

_F32 = jnp.float32
_BF16 = jnp.bfloat16
_MIB = 1024 * 1024


def _rms_norm(x, gain):
    ms = jnp.mean(x * x, axis=-1, keepdims=True)
    return x * lax.rsqrt(ms + RMS_EPS) * gain


def _dot(a, b):
    return jnp.dot(a, b, preferred_element_type=_F32)


def _ffn_kernel(x_ref, g_ref, wg_ref, wu_ref, wd_ref, *rest, final_norm):
    if final_norm:
        gf_ref, o_ref = rest
    else:
        (o_ref,) = rest
    for sub in range(x_ref.shape[0] // FFN_SUB):
        rows = slice(sub * FFN_SUB, (sub + 1) * FFN_SUB)
        x = x_ref[rows, :]
        n = _rms_norm(x, g_ref[...]).astype(_BF16)
        hidden = []
        for c in range(D_FF // FF_CHUNK):
            cols = slice(c * FF_CHUNK, (c + 1) * FF_CHUNK)
            gate = _dot(n, wg_ref[:, cols])
            up = _dot(n, wu_ref[:, cols])
            hidden.append((gate * jax.nn.sigmoid(gate) * up).astype(_BF16))
        h = jnp.concatenate(hidden, axis=1)
        out = x + 0.5 * _dot(h, wd_ref[...])
        if final_norm:
            out = _rms_norm(out, gf_ref[...])
        o_ref[rows, :] = out


def _resident(shape):
    zeros = (0,) * len(shape)
    return pl.BlockSpec(shape, lambda *_: zeros, pipeline_mode=pl.Buffered(1))


def _ffn_call(x2d, gain, wg, wu, wd, final_gain=None):
    rows = x2d.shape[0]
    tm = FFN_TILE
    final_norm = final_gain is not None
    row_spec = pl.BlockSpec((tm, D_MODEL), lambda i: (i, 0))
    in_specs = [row_spec, _resident((1, D_MODEL)), _resident((D_MODEL, D_FF)),
                _resident((D_MODEL, D_FF)), _resident((D_FF, D_MODEL))]
    args = [x2d, gain.reshape(1, D_MODEL), wg, wu, wd]
    if final_norm:
        in_specs.append(_resident((1, D_MODEL)))
        args.append(final_gain.reshape(1, D_MODEL))
    weight_bytes = 3 * D_MODEL * D_FF * 2
    io_bytes = 4 * tm * D_MODEL * 4
    temp_bytes = tm * D_FF * 2 * 2 + tm * D_MODEL * (2 + 4 + 4) + 8 * FFN_SUB * FF_CHUNK * 4
    return pl.pallas_call(
        functools.partial(_ffn_kernel, final_norm=final_norm),
        out_shape=jax.ShapeDtypeStruct((rows, D_MODEL), _F32),
        grid=(rows // tm,),
        in_specs=in_specs,
        out_specs=row_spec,
        compiler_params=pltpu.CompilerParams(
            dimension_semantics=("arbitrary",),
            vmem_limit_bytes=weight_bytes + io_bytes + temp_bytes + 4 * _MIB),
        name="ffn_final" if final_norm else "ffn",
    )(*args)


def _mixer_tables(seq):
    heads = np.arange(RET_HEADS)
    gamma = 1.0 - 2.0 ** (-5.0 - heads)
    i = np.arange(RET_BLOCK)
    diff = i[:, None] - i[None, :]
    same = (i[:, None] // CHUNK) == (i[None, :] // CHUNK)
    earlier = (i[None, :] // CHUNK) < (i[:, None] // CHUNK)
    expo = np.where(same, np.abs(diff), diff)
    q_scale = RET_DK ** -0.5
    dmat = np.where((same | earlier)[None], gamma[:, None, None] ** expo[None], 0.0) * q_scale
    dq = np.repeat((gamma[None, :] ** (i[:, None] + 1.0)) * q_scale, RET_DV, axis=1)
    dk = np.repeat(gamma[None, :] ** (RET_BLOCK - 1.0 - i[:, None]), RET_DK, axis=1)
    dc = np.repeat(gamma ** float(RET_BLOCK), RET_DV)[None, :]
    bm = (np.arange(RET_QKWIDTH)[:, None] // RET_DK ==
          np.arange(RET_VWIDTH)[None, :] // RET_DV).astype(np.float64)
    half = RET_DK // 2
    freqs = ROPE_BASE ** (-np.arange(half) * 2.0 / RET_DK)
    ang = np.arange(seq)[:, None] * freqs[None, :]
    cos = np.tile(np.cos(ang), (1, 4))
    sin = np.tile(np.concatenate([-np.sin(ang), np.sin(ang)], axis=1), (1, 2))
    f32 = lambda a: jnp.asarray(a, dtype=_F32)
    return tuple(f32(a) for a in (cos, sin, dmat, dq, dk, dc, bm))


def _rotary(t, cos, sin):
    lane = lax.broadcasted_iota(jnp.int32, (t.shape[0], 128), 1)
    first_half = (lane & (RET_DK - 1)) < (RET_DK // 2)
    cols = []
    for c in range(t.shape[1] // 128):
        tc = t[:, c * 128:(c + 1) * 128]
        partner = jnp.where(first_half, pltpu.roll(tc, 128 - RET_DK // 2, 1),
                            pltpu.roll(tc, RET_DK // 2, 1))
        cols.append(tc * cos + partner * sin)
    return jnp.concatenate(cols, axis=1)


def _mixer_kernel(x_ref, g_ref, win_ref, wu_ref, wout_ref, gain_ref, pw_ref, ps_ref,
                  cos_ref, sin_ref, dmat_ref, dq_ref, dk_ref, dc_ref, bm_ref,
                  o_ref, state_ref, u_ref, wup_ref, *, tm):
    s = pl.program_id(1)
    o1, o2, o3, o4 = RET_QKWIDTH, 2 * RET_QKWIDTH, 2 * RET_QKWIDTH + RET_VWIDTH, IN_WIDTH - POOL_WIDTH

    @pl.when(s == 0)
    def _():
        state_ref[...] = jnp.zeros_like(state_ref)
        u_ref[0:POOL_HALO, :] = jnp.zeros((POOL_HALO, POOL_WIDTH), _F32)

    @pl.when(s > 0)
    def _():
        u_ref[0:POOL_HALO, :] = u_ref[tm:tm + POOL_HALO, :]

    @pl.when((pl.program_id(0) == 0) & (s == 0))
    def _():
        for gi in range(len(POOL_WINDOWS)):
            cs = slice(gi * POOL_GC, (gi + 1) * POOL_GC)
            wup_ref[:, cs] = jnp.dot(wu_ref[:, cs], pw_ref[gi], preferred_element_type=_F32,
                                     precision=lax.Precision.HIGHEST).astype(_BF16)

    x = x_ref[...]
    n = _rms_norm(x, g_ref[...]).astype(_BF16)
    row0 = pl.multiple_of(s * tm, tm)

    u = _dot(n, wup_ref[...])
    u_ref[POOL_HALO:POOL_HALO + tm, :] = u
    pos = row0 + lax.broadcasted_iota(jnp.int32, (tm, 1), 0)
    pooled = []
    for gi, w in enumerate(POOL_WINDOWS):
        cs = slice(gi * POOL_GC, (gi + 1) * POOL_GC)
        acc = u_ref[:, cs]
        shift = 1
        while shift < w:
            acc = acc + pltpu.roll(acc, shift, 0)
            shift *= 2
        cnt = jnp.minimum(pos + 1, w).astype(_F32)
        pooled.append(acc[POOL_HALO:, :] / cnt - u[:, cs])
    pool = (jnp.concatenate(pooled, axis=1) * ps_ref[...]).astype(_BF16)

    gate = _dot(n, win_ref[:, o3:o4])
    gate = gate * jax.nn.sigmoid(gate)
    qkv = _dot(n, win_ref[:, :o3])

    cos = cos_ref[pl.ds(row0, tm), :]
    sin = sin_ref[pl.ds(row0, tm), :]
    q = _rotary(qkv[:, :o1], cos, sin).astype(_BF16)
    k = _rotary(qkv[:, o1:o2], cos, sin)
    v = qkv[:, o2:o3].astype(_BF16)

    lane_head = lax.broadcasted_iota(jnp.int32, (RET_BLOCK, RET_QKWIDTH), 1) // RET_DK
    blocks = [slice(b * RET_BLOCK, (b + 1) * RET_BLOCK) for b in range(tm // RET_BLOCK)]
    states = [state_ref[...]]
    intra_blocks = []
    for rows in blocks:
        qb, kb, vb = q[rows], k[rows], v[rows]
        decayed = []
        for h in range(RET_HEADS):
            kh = jnp.where(lane_head == h, kb, 0.0).astype(_BF16)
            sc = lax.dot_general(qb, kh, (((1,), (1,)), ((), ())),
                                 preferred_element_type=_F32)
            decayed.append((sc * dmat_ref[h]).astype(_BF16))
        kd = (kb * dk_ref[...]).astype(_BF16)
        kv = lax.dot_general(kd, vb, (((0,), (0,)), ((), ())),
                             preferred_element_type=_F32)
        states.append(states[-1] * dc_ref[...] + kv * bm_ref[...])
        intra_blocks.append(jnp.concatenate(
            [_dot(decayed[h], vb[:, h * RET_DV:(h + 1) * RET_DV]) for h in range(RET_HEADS)],
            axis=1))
    state_ref[...] = states[-1]
    ret_blocks = []
    for rows, state, intra in zip(blocks, states, intra_blocks):
        cross = _dot(q[rows], state.astype(_BF16)) * dq_ref[...]
        ret_blocks.append(intra + cross)
    ret = ret_blocks[0] if len(ret_blocks) == 1 else jnp.concatenate(ret_blocks, axis=0)

    gn = []
    for h in range(RET_HEADS):
        o_h = ret[:, h * RET_DV:(h + 1) * RET_DV]
        mu = jnp.mean(o_h, axis=-1, keepdims=True)
        d = o_h - mu
        var = jnp.mean(d * d, axis=-1, keepdims=True)
        gn.append(d * lax.rsqrt(var + GN_EPS))
    r = gate * (jnp.concatenate(gn, axis=1) * gain_ref[...])

    mix = (_dot(r.astype(_BF16), wout_ref[0:RET_VWIDTH, :]) +
           _dot(pool, wout_ref[RET_VWIDTH:, :]))
    o_ref[...] = x + mix


def _mixer_call(x2d, batch, seq, gain, w_in, w_out, gn_gain, pool_w, pool_scale):
    tm = MIX_TILE
    steps = seq // tm
    tables = _mixer_tables(seq)
    row_spec = pl.BlockSpec((tm, D_MODEL), lambda b, s: (b * steps + s, 0))
    o4 = IN_WIDTH - POOL_WIDTH
    small = [gain.reshape(1, D_MODEL), w_in[:, :o4].astype(_BF16), w_in[:, o4:],
             w_out.astype(_BF16), gn_gain.reshape(1, RET_VWIDTH),
             pool_w, pool_scale.reshape(1, POOL_WIDTH), *tables]
    in_specs = [row_spec] + [_resident(a.shape) for a in small]
    resident_bytes = sum(int(np.prod(a.shape)) * a.dtype.itemsize for a in small)
    temp_bytes = tm * (IN_WIDTH * 4 * 2 + D_MODEL * 4 * 6) + 4 * RET_BLOCK * RET_BLOCK * 8
    return pl.pallas_call(
        functools.partial(_mixer_kernel, tm=tm),
        out_shape=jax.ShapeDtypeStruct(x2d.shape, _F32),
        grid=(batch, steps),
        in_specs=in_specs,
        out_specs=row_spec,
        scratch_shapes=[pltpu.VMEM((RET_QKWIDTH, RET_VWIDTH), _F32),
                        pltpu.VMEM((POOL_HALO + tm, POOL_WIDTH), _F32),
                        pltpu.VMEM((D_MODEL, POOL_WIDTH), _BF16)],
        compiler_params=pltpu.CompilerParams(
            dimension_semantics=("arbitrary", "arbitrary"),
            vmem_limit_bytes=resident_bytes + 4 * tm * D_MODEL * 4 + temp_bytes + 8 * _MIB),
        name="mixer",
    )(x2d, *small)


def kernel(x, norm_ffn1, ffn1_gate, ffn1_up, ffn1_down, norm_mix, w_in, ret_gn_gain, pool_w,
           pool_scale, w_out, norm_ffn2, ffn2_gate, ffn2_up, ffn2_down, norm_final):
    batch, seq, d_model = x.shape
    assert d_model == D_MODEL and seq % MIX_TILE == 0 and (batch * seq) % FFN_TILE == 0
    bf = lambda a: a.astype(_BF16)
    h = x.reshape(batch * seq, d_model)
    depth = norm_ffn1.shape[0]
    for l in range(depth):
        last = l == depth - 1
        h = _ffn_call(h, norm_ffn1[l], bf(ffn1_gate[l]), bf(ffn1_up[l]), bf(ffn1_down[l]))
        h = _mixer_call(h, batch, seq, norm_mix[l], w_in[l], w_out[l], ret_gn_gain[l],
                        pool_w[l], pool_scale[l])
        h = _ffn_call(h, norm_ffn2[l], bf(ffn2_gate[l]), bf(ffn2_up[l]), bf(ffn2_down[l]),
                      final_gain=norm_final if last else None)
    return h.reshape(batch, seq, d_model)
```

```python
import functools

import numpy as np
import jax
import jax.numpy as jnp
from jax import lax
from jax.experimental import pallas as pl
from jax.experimental.pallas import tpu as pltpu

D_MODEL = 1024
D_FF = 2816
CHUNK = 64
RET_HEADS = 4
RET_DK = 64
RET_DV = 128
RET_QKWIDTH = RET_HEADS * RET_DK
RET_VWIDTH = RET_HEADS * RET_DV
POOL_WINDOWS = (2, 4, 8, 16)
POOL_GC = 128
POOL_WIDTH = len(POOL_WINDOWS) * POOL_GC
POOL_HALO = 16
IN_WIDTH = 2 * RET_QKWIDTH + 2 * RET_VWIDTH + POOL_WIDTH
ROPE_BASE = 10000.0
RMS_EPS = 1e-6
GN_EPS = 1e-5

RET_BLOCK = 256
FFN_TILE = 1024
FFN_SUB = 512
FF_CHUNK = 256
MIX_TILE = 1024
WEIGHT_CHUNKS = 8

_F32 = jnp.float32
_BF16 = jnp.bfloat16
_MIB = 1024 * 1024


def _rms_norm(x, gain):
    ms = jnp.mean(x * x, axis=-1, keepdims=True)
    return x * lax.rsqrt(ms + RMS_EPS) * gain


def _dot(a, b):
    return jnp.dot(a, b, preferred_element_type=_F32)


def _load_weights_bf16(pairs, stage, sem):
    chunk_rows = stage.shape[1]
    jobs = [(src, dst, c) for src, dst in pairs for c in range(src.shape[0] // chunk_rows)]

    def chunk_copy(j):
        src, _, c = jobs[j]
        return pltpu.make_async_copy(src.at[pl.ds(c * chunk_rows, chunk_rows), :],
                                     stage.at[j % 2], sem.at[j % 2])

    chunk_copy(0).start()
    for j, (_, dst, c) in enumerate(jobs):
        if j + 1 < len(jobs):
            chunk_copy(j + 1).start()
        chunk_copy(j).wait()
        dst[c * chunk_rows:(c + 1) * chunk_rows, :] = stage[j % 2].astype(_BF16)


def _weight_scratch(shapes):
    homes = [pltpu.VMEM(s, _BF16) for s in shapes]
    rings = []
    for s in dict.fromkeys(shapes):
        rings += [pltpu.VMEM((2, s[0] // WEIGHT_CHUNKS, s[1]), _F32), pltpu.SemaphoreType.DMA((2,))]
    return homes + rings


def _weight_scratch_bytes(shapes):
    homes = sum(2 * s[0] * s[1] for s in shapes)
    rings = sum(2 * 4 * (s[0] // WEIGHT_CHUNKS) * s[1] for s in dict.fromkeys(shapes))
    return homes + rings


def _ffn_kernel(x_ref, g_ref, wg_hbm, wu_hbm, wd_hbm, *rest, final_norm):
    if final_norm:
        gf_ref, *rest = rest
    o_ref, wg_ref, wu_ref, wd_ref, up_stage, up_sem, down_stage, down_sem = rest

    @pl.when(pl.program_id(0) == 0)
    def _():
        _load_weights_bf16([(wg_hbm, wg_ref), (wu_hbm, wu_ref)], up_stage, up_sem)
        _load_weights_bf16([(wd_hbm, wd_ref)], down_stage, down_sem)

    for sub in range(x_ref.shape[0] // FFN_SUB):
        rows = slice(sub * FFN_SUB, (sub + 1) * FFN_SUB)
        x = x_ref[rows, :]
        n = _rms_norm(x, g_ref[...]).astype(_BF16)
        hidden = []
        for c in range(D_FF // FF_CHUNK):
            cols = slice(c * FF_CHUNK, (c + 1) * FF_CHUNK)
            gate = _dot(n, wg_ref[:, cols])
            up = _dot(n, wu_ref[:, cols])
            hidden.append((gate * jax.nn.sigmoid(gate) * up).astype(_BF16))
        h = jnp.concatenate(hidden, axis=1)
        out = x + 0.5 * _dot(h, wd_ref[...])
        if final_norm:
            out = _rms_norm(out, gf_ref[...])
        o_ref[rows, :] = out


def _resident(shape):
    zeros = (0,) * len(shape)
    return pl.BlockSpec(shape, lambda *_: zeros, pipeline_mode=pl.Buffered(1))


_IN_HBM = pl.BlockSpec(memory_space=pl.ANY)


def _ffn_call(x2d, gain, wg, wu, wd, final_gain=None):
    rows = x2d.shape[0]
    tm = FFN_TILE
    final_norm = final_gain is not None
    row_spec = pl.BlockSpec((tm, D_MODEL), lambda i: (i, 0))
    in_specs = [row_spec, _resident((1, D_MODEL)), _IN_HBM, _IN_HBM, _IN_HBM]
    args = [x2d, gain.reshape(1, D_MODEL), wg, wu, wd]
    if final_norm:
        in_specs.append(_resident((1, D_MODEL)))
        args.append(final_gain.reshape(1, D_MODEL))
    weight_shapes = [wg.shape, wu.shape, wd.shape]
    io_bytes = 4 * tm * D_MODEL * 4
    temp_bytes = tm * D_FF * 2 * 2 + tm * D_MODEL * (2 + 4 + 4) + 8 * FFN_SUB * FF_CHUNK * 4
    return pl.pallas_call(
        functools.partial(_ffn_kernel, final_norm=final_norm),
        out_shape=jax.ShapeDtypeStruct((rows, D_MODEL), _F32),
        grid=(rows // tm,),
        in_specs=in_specs,
        out_specs=row_spec,
        scratch_shapes=_weight_scratch(weight_shapes),
        compiler_params=pltpu.CompilerParams(
            dimension_semantics=("arbitrary",),
            vmem_limit_bytes=(_weight_scratch_bytes(weight_shapes) + io_bytes + temp_bytes +
                              4 * _MIB)),
        name="ffn_final" if final_norm else "ffn",
    )(*args)


def _mixer_tables(seq):
    heads = np.arange(RET_HEADS)
    gamma = 1.0 - 2.0 ** (-5.0 - heads)
    i = np.arange(RET_BLOCK)
    diff = i[:, None] - i[None, :]
    same = (i[:, None] // CHUNK) == (i[None, :] // CHUNK)
    earlier = (i[None, :] // CHUNK) < (i[:, None] // CHUNK)
    expo = np.where(same, np.abs(diff), diff)
    q_scale = RET_DK ** -0.5
    dmat = np.where((same | earlier)[None], gamma[:, None, None] ** expo[None], 0.0) * q_scale
    dq = np.repeat((gamma[None, :] ** (i[:, None] + 1.0)) * q_scale, RET_DV, axis=1)
    dk = np.repeat(gamma[None, :] ** (RET_BLOCK - 1.0 - i[:, None]), RET_DK, axis=1)
    dc = np.repeat(gamma ** float(RET_BLOCK), RET_DV)[None, :]
    bm = (np.arange(RET_QKWIDTH)[:, None] // RET_DK ==
          np.arange(RET_VWIDTH)[None, :] // RET_DV).astype(np.float64)
    half = RET_DK // 2
    freqs = ROPE_BASE ** (-np.arange(half) * 2.0 / RET_DK)
    ang = np.arange(seq)[:, None] * freqs[None, :]
    cos = np.tile(np.cos(ang), (1, 4))
    sin = np.tile(np.concatenate([-np.sin(ang), np.sin(ang)], axis=1), (1, 2))
    f32 = lambda a: jnp.asarray(a, dtype=_F32)
    return tuple(f32(a) for a in (cos, sin, dmat, dq, dk, dc, bm))


def _rotary(t, cos, sin):
    lane = lax.broadcasted_iota(jnp.int32, (t.shape[0], 128), 1)
    first_half = (lane & (RET_DK - 1)) < (RET_DK // 2)
    cols = []
    for c in range(t.shape[1] // 128):
        tc = t[:, c * 128:(c + 1) * 128]
        partner = jnp.where(first_half, pltpu.roll(tc, 128 - RET_DK // 2, 1),
                            pltpu.roll(tc, RET_DK // 2, 1))
        cols.append(tc * cos + partner * sin)
    return jnp.concatenate(cols, axis=1)


def _mixer_kernel(x_ref, g_ref, win_hbm, wout_hbm, gain_ref, pw_ref, ps_ref,
                  cos_ref, sin_ref, dmat_ref, dq_ref, dk_ref, dc_ref, bm_ref,
                  o_ref, state_ref, u_ref, win_ref, wout_ref, in_stage, in_sem, out_stage, out_sem,
                  *, tm):
    s = pl.program_id(1)
    o1, o2, o3, o4 = RET_QKWIDTH, 2 * RET_QKWIDTH, 2 * RET_QKWIDTH + RET_VWIDTH, IN_WIDTH - POOL_WIDTH

    @pl.when((pl.program_id(0) == 0) & (s == 0))
    def _():
        _load_weights_bf16([(win_hbm, win_ref)], in_stage, in_sem)
        _load_weights_bf16([(wout_hbm, wout_ref)], out_stage, out_sem)

    @pl.when(s == 0)
    def _():
        state_ref[...] = jnp.zeros_like(state_ref)
        u_ref[0:POOL_HALO, :] = jnp.zeros((POOL_HALO, POOL_WIDTH), _F32)

    @pl.when(s > 0)
    def _():
        u_ref[0:POOL_HALO, :] = u_ref[tm:tm + POOL_HALO, :]

    x = x_ref[...]
    n = _rms_norm(x, g_ref[...]).astype(_BF16)
    row0 = pl.multiple_of(s * tm, tm)

    u = _dot(n, win_ref[:, o4:])
    u_ref[POOL_HALO:POOL_HALO + tm, :] = u
    pos = row0 + lax.broadcasted_iota(jnp.int32, (tm, 1), 0)
    pooled = []
    for gi, w in enumerate(POOL_WINDOWS):
        cs = slice(gi * POOL_GC, (gi + 1) * POOL_GC)
        acc = u_ref[:, cs]
        shift = 1
        while shift < w:
            acc = acc + pltpu.roll(acc, shift, 0)
            shift *= 2
        cnt = jnp.minimum(pos + 1, w).astype(_F32)
        pooled.append((acc[POOL_HALO:, :] / cnt - u[:, cs]).astype(_BF16))

    gate = _dot(n, win_ref[:, o3:o4])
    gate = gate * jax.nn.sigmoid(gate)
    qkv = _dot(n, win_ref[:, :o3])
    pool = jnp.concatenate([_dot(pooled[gi], pw_ref[gi]) for gi in range(len(POOL_WINDOWS))],
                           axis=1)
    pool = (pool * ps_ref[...]).astype(_BF16)

    cos = cos_ref[pl.ds(row0, tm), :]
    sin = sin_ref[pl.ds(row0, tm), :]
    q = _rotary(qkv[:, :o1], cos, sin).astype(_BF16)
    k = _rotary(qkv[:, o1:o2], cos, sin)
    v = qkv[:, o2:o3].astype(_BF16)

    lane_head = lax.broadcasted_iota(jnp.int32, (RET_BLOCK, RET_QKWIDTH), 1) // RET_DK
    blocks = [slice(b * RET_BLOCK, (b + 1) * RET_BLOCK) for b in range(tm // RET_BLOCK)]
    states = [state_ref[...]]
    intra_blocks = []
    for rows in blocks:
        qb, kb, vb = q[rows], k[rows], v[rows]
        decayed = []
        for h in range(RET_HEADS):
            kh = jnp.where(lane_head == h, kb, 0.0).astype(_BF16)
            sc = lax.dot_general(qb, kh, (((1,), (1,)), ((), ())),
                                 preferred_element_type=_F32)
            decayed.append((sc * dmat_ref[h]).astype(_BF16))
        kd = (kb * dk_ref[...]).astype(_BF16)
        kv = lax.dot_general(kd, vb, (((0,), (0,)), ((), ())),
                             preferred_element_type=_F32)
        states.append(states[-1] * dc_ref[...] + kv * bm_ref[...])
        intra_blocks.append(jnp.concatenate(
            [_dot(decayed[h], vb[:, h * RET_DV:(h + 1) * RET_DV]) for h in range(RET_HEADS)],
            axis=1))
    state_ref[...] = states[-1]
    ret_blocks = []
    for rows, state, intra in zip(blocks, states, intra_blocks):
        cross = _dot(q[rows], state.astype(_BF16)) * dq_ref[...]
        ret_blocks.append(intra + cross)
    ret = ret_blocks[0] if len(ret_blocks) == 1 else jnp.concatenate(ret_blocks, axis=0)

    gn = []
    for h in range(RET_HEADS):
        o_h = ret[:, h * RET_DV:(h + 1) * RET_DV]
        mu = jnp.mean(o_h, axis=-1, keepdims=True)
        d = o_h - mu
        var = jnp.mean(d * d, axis=-1, keepdims=True)
        gn.append(d * lax.rsqrt(var + GN_EPS))
    r = gate * (jnp.concatenate(gn, axis=1) * gain_ref[...])

    mix = (_dot(r.astype(_BF16), wout_ref[0:RET_VWIDTH, :]) +
           _dot(pool, wout_ref[RET_VWIDTH:, :]))
    o_ref[...] = x + mix


def _mixer_call(x2d, batch, seq, gain, w_in, w_out, gn_gain, pool_w, pool_scale):
    tm = MIX_TILE
    steps = seq // tm
    row_spec = pl.BlockSpec((tm, D_MODEL), lambda b, s: (b * steps + s, 0))
    small = [gn_gain.reshape(1, RET_VWIDTH), pool_w.astype(_BF16),
             pool_scale.reshape(1, POOL_WIDTH), *_mixer_tables(seq)]
    in_specs = ([row_spec, _resident((1, D_MODEL)), _IN_HBM, _IN_HBM] +
                [_resident(a.shape) for a in small])
    weight_shapes = [w_in.shape, w_out.shape]
    resident_bytes = (sum(int(np.prod(a.shape)) * a.dtype.itemsize for a in small) +
                      _weight_scratch_bytes(weight_shapes))
    temp_bytes = tm * (IN_WIDTH * 4 * 2 + D_MODEL * 4 * 6) + 4 * RET_BLOCK * RET_BLOCK * 8
    return pl.pallas_call(
        functools.partial(_mixer_kernel, tm=tm),
        out_shape=jax.ShapeDtypeStruct(x2d.shape, _F32),
        grid=(batch, steps),
        in_specs=in_specs,
        out_specs=row_spec,
        scratch_shapes=[pltpu.VMEM((RET_QKWIDTH, RET_VWIDTH), _F32),
                        pltpu.VMEM((POOL_HALO + tm, POOL_WIDTH), _F32),
                        *_weight_scratch(weight_shapes)],
        compiler_params=pltpu.CompilerParams(
            dimension_semantics=("arbitrary", "arbitrary"),
            vmem_limit_bytes=resident_bytes + 4 * tm * D_MODEL * 4 + temp_bytes + 8 * _MIB),
        name="mixer",
    )(x2d, gain.reshape(1, D_MODEL), w_in, w_out, *small)


def kernel(x, norm_ffn1, ffn1_gate, ffn1_up, ffn1_down, norm_mix, w_in, ret_gn_gain, pool_w,
           pool_scale, w_out, norm_ffn2, ffn2_gate, ffn2_up, ffn2_down, norm_final):
    batch, seq, d_model = x.shape
    assert d_model == D_MODEL and seq % MIX_TILE == 0 and (batch * seq) % FFN_TILE == 0
    h = x.reshape(batch * seq, d_model)
    depth = norm_ffn1.shape[0]
    for l in range(depth):
        last = l == depth - 1
        h = _ffn_call(h, norm_ffn1[l], ffn1_gate[l], ffn1_up[l], ffn1_down[l])
        h = _mixer_call(h, batch, seq, norm_mix[l], w_in[l], w_out[l], ret_gn_gain[l],
                        pool_w[l], pool_scale[l])
        h = _ffn_call(h, norm_ffn2[l], ffn2_gate[l], ffn2_up[l], ffn2_down[l],
                      final_gain=norm_final if last else None)
    return h.reshape(batch, seq, d_model)
```

```python
import functools

import numpy as np
import jax
import jax.numpy as jnp
from jax import lax
from jax.experimental import pallas as pl
from jax.experimental.pallas import tpu as pltpu

D_MODEL = 1024
D_FF = 2816
CHUNK = 64
RET_HEADS = 4
RET_DK = 64
RET_DV = 128
RET_QKWIDTH = RET_HEADS * RET_DK
RET_VWIDTH = RET_HEADS * RET_DV
POOL_WINDOWS = (2, 4, 8, 16)
POOL_GC = 128
POOL_WIDTH = len(POOL_WINDOWS) * POOL_GC
POOL_HALO = 16
IN_WIDTH = 2 * RET_QKWIDTH + 2 * RET_VWIDTH + POOL_WIDTH
ROPE_BASE = 10000.0
RMS_EPS = 1e-6
GN_EPS = 1e-5

RET_BLOCK = 256
FFN_TILE = 1024
FFN_SUB = 512
FF_CHUNK = 256
MIX_TILE = 1024

_F32 = jnp.float32
_BF16 = jnp.bfloat16
_MIB = 1024 * 1024


def _rms_scale(x):
    return lax.rsqrt(jnp.mean(x * x, axis=-1, keepdims=True) + RMS_EPS)


def _rms_norm(x, gain):
    return x * _rms_scale(x) * gain


def _dot(a, b):
    return jnp.dot(a, b, preferred_element_type=_F32)


def _ffn_kernel(x_ref, g_ref, wg_ref, wu_ref, wd_ref, *rest, final_norm):
    if final_norm:
        gf_ref, o_ref = rest
    else:
        (o_ref,) = rest
    for sub in range(x_ref.shape[0] // FFN_SUB):
        rows = slice(sub * FFN_SUB, (sub + 1) * FFN_SUB)
        x = x_ref[rows, :]
        xg = (x * g_ref[...]).astype(_BF16)
        r = jnp.broadcast_to(_rms_scale(x), (FFN_SUB, FF_CHUNK))
        hidden = []
        for c in range(D_FF // FF_CHUNK):
            cols = slice(c * FF_CHUNK, (c + 1) * FF_CHUNK)
            gate = _dot(xg, wg_ref[:, cols]) * r
            up = _dot(xg, wu_ref[:, cols]) * r
            hidden.append((gate * jax.nn.sigmoid(gate) * up).astype(_BF16))
        h = jnp.concatenate(hidden, axis=1)
        out = x + 0.5 * _dot(h, wd_ref[...])
        if final_norm:
            out = _rms_norm(out, gf_ref[...])
        o_ref[rows, :] = out


def _resident(shape):
    zeros = (0,) * len(shape)
    return pl.BlockSpec(shape, lambda *_: zeros, pipeline_mode=pl.Buffered(1))


def _ffn_call(x2d, gain, wg, wu, wd, final_gain=None):
    rows = x2d.shape[0]
    tm = FFN_TILE
    final_norm = final_gain is not None
    row_spec = pl.BlockSpec((tm, D_MODEL), lambda i: (i, 0))
    in_specs = [row_spec, _resident((1, D_MODEL)), _resident((D_MODEL, D_FF)),
                _resident((D_MODEL, D_FF)), _resident((D_FF, D_MODEL))]
    args = [x2d, gain.reshape(1, D_MODEL), wg, wu, wd]
    if final_norm:
        in_specs.append(_resident((1, D_MODEL)))
        args.append(final_gain.reshape(1, D_MODEL))
    weight_bytes = 3 * D_MODEL * D_FF * 2
    io_bytes = 4 * tm * D_MODEL * 4
    temp_bytes = tm * D_FF * 2 * 2 + tm * D_MODEL * (2 + 4 + 4) + 8 * FFN_SUB * FF_CHUNK * 4
    return pl.pallas_call(
        functools.partial(_ffn_kernel, final_norm=final_norm),
        out_shape=jax.ShapeDtypeStruct((rows, D_MODEL), _F32),
        grid=(rows // tm,),
        in_specs=in_specs,
        out_specs=row_spec,
        compiler_params=pltpu.CompilerParams(
            dimension_semantics=("arbitrary",),
            vmem_limit_bytes=weight_bytes + io_bytes + temp_bytes + 4 * _MIB),
        name="ffn_final" if final_norm else "ffn",
    )(*args)


def _mixer_tables(seq):
    heads = np.arange(RET_HEADS)
    gamma = 1.0 - 2.0 ** (-5.0 - heads)
    i = np.arange(RET_BLOCK)
    diff = i[:, None] - i[None, :]
    same = (i[:, None] // CHUNK) == (i[None, :] // CHUNK)
    earlier = (i[None, :] // CHUNK) < (i[:, None] // CHUNK)
    expo = np.where(same, np.abs(diff), diff)
    q_scale = RET_DK ** -0.5
    dmat = np.where((same | earlier)[None], gamma[:, None, None] ** expo[None], 0.0) * q_scale
    dq = np.repeat((gamma[None, :] ** (i[:, None] + 1.0)) * q_scale, RET_DV, axis=1)
    dk = np.repeat(gamma[None, :] ** (RET_BLOCK - 1.0 - i[:, None]), RET_DK, axis=1)
    dc = np.repeat(gamma ** float(RET_BLOCK), RET_DV)[None, :]
    bm = (np.arange(RET_QKWIDTH)[:, None] // RET_DK ==
          np.arange(RET_VWIDTH)[None, :] // RET_DV).astype(np.float64)
    half = RET_DK // 2
    freqs = ROPE_BASE ** (-np.arange(half) * 2.0 / RET_DK)
    ang = np.arange(seq)[:, None] * freqs[None, :]
    cos = np.tile(np.cos(ang), (1, 4))
    sin = np.tile(np.concatenate([-np.sin(ang), np.sin(ang)], axis=1), (1, 2))
    f32 = lambda a: jnp.asarray(a, dtype=_F32)
    return tuple(f32(a) for a in (cos, sin, dmat, dq, dk, dc, bm))


def _rotary(t, cos, sin):
    lane = lax.broadcasted_iota(jnp.int32, (t.shape[0], 128), 1)
    first_half = (lane & (RET_DK - 1)) < (RET_DK // 2)
    cols = []
    for c in range(t.shape[1] // 128):
        tc = t[:, c * 128:(c + 1) * 128]
        partner = jnp.where(first_half, pltpu.roll(tc, 128 - RET_DK // 2, 1),
                            pltpu.roll(tc, RET_DK // 2, 1))
        cols.append(tc * cos + partner * sin)
    return jnp.concatenate(cols, axis=1)


def _mixer_kernel(x_ref, g_ref, win_ref, wout_ref, gain_ref, pw_ref, ps_ref,
                  cos_ref, sin_ref, dmat_ref, dq_ref, dk_ref, dc_ref, bm_ref,
                  o_ref, state_ref, u_ref, *, tm):
    s = pl.program_id(1)
    o1, o2, o3, o4 = RET_QKWIDTH, 2 * RET_QKWIDTH, 2 * RET_QKWIDTH + RET_VWIDTH, IN_WIDTH - POOL_WIDTH

    @pl.when(s == 0)
    def _():
        state_ref[...] = jnp.zeros_like(state_ref)
        u_ref[0:POOL_HALO, :] = jnp.zeros((POOL_HALO, POOL_WIDTH), _F32)

    @pl.when(s > 0)
    def _():
        u_ref[0:POOL_HALO, :] = u_ref[tm:tm + POOL_HALO, :]

    x = x_ref[...]
    n = (x * g_ref[...]).astype(_BF16)
    r_in = jnp.broadcast_to(_rms_scale(x), (tm, POOL_WIDTH))
    row0 = pl.multiple_of(s * tm, tm)

    u = _dot(n, win_ref[:, o4:]) * r_in
    u_ref[POOL_HALO:POOL_HALO + tm, :] = u
    pos = row0 + lax.broadcasted_iota(jnp.int32, (tm, 1), 0)
    pooled = []
    for gi, w in enumerate(POOL_WINDOWS):
        cs = slice(gi * POOL_GC, (gi + 1) * POOL_GC)
        acc = u_ref[:, cs]
        shift = 1
        while shift < w:
            acc = acc + pltpu.roll(acc, shift, 0)
            shift *= 2
        cnt = jnp.minimum(pos + 1, w).astype(_F32)
        pooled.append((acc[POOL_HALO:, :] / cnt - u[:, cs]).astype(_BF16))

    gate = _dot(n, win_ref[:, o3:o4]) * r_in
    gate = gate * jax.nn.sigmoid(gate)
    qkv = _dot(n, win_ref[:, :o3]) * jnp.concatenate([r_in, r_in], axis=1)
    pool = jnp.concatenate([_dot(pooled[gi], pw_ref[gi]) for gi in range(len(POOL_WINDOWS))],
                           axis=1)
    pool = (pool * ps_ref[...]).astype(_BF16)

    cos = cos_ref[pl.ds(row0, tm), :]
    sin = sin_ref[pl.ds(row0, tm), :]
    q = _rotary(qkv[:, :o1], cos, sin).astype(_BF16)
    k = _rotary(qkv[:, o1:o2], cos, sin)
    v = qkv[:, o2:o3].astype(_BF16)

    lane_head = lax.broadcasted_iota(jnp.int32, (RET_BLOCK, RET_QKWIDTH), 1) // RET_DK
    blocks = [slice(b * RET_BLOCK, (b + 1) * RET_BLOCK) for b in range(tm // RET_BLOCK)]
    states = [state_ref[...]]
    intra_blocks = []
    for rows in blocks:
        qb, kb, vb = q[rows], k[rows], v[rows]
        decayed = []
        for h in range(RET_HEADS):
            kh = jnp.where(lane_head == h, kb, 0.0).astype(_BF16)
            sc = lax.dot_general(qb, kh, (((1,), (1,)), ((), ())),
                                 preferred_element_type=_F32)
            decayed.append((sc * dmat_ref[h]).astype(_BF16))
        kd = (kb * dk_ref[...]).astype(_BF16)
        kv = lax.dot_general(kd, vb, (((0,), (0,)), ((), ())),
                             preferred_element_type=_F32)
        states.append(states[-1] * dc_ref[...] + kv * bm_ref[...])
        intra_blocks.append(jnp.concatenate(
            [_dot(decayed[h], vb[:, h * RET_DV:(h + 1) * RET_DV]) for h in range(RET_HEADS)],
            axis=1))
    state_ref[...] = states[-1]
    ret_blocks = []
    for rows, state, intra in zip(blocks, states, intra_blocks):
        cross = _dot(q[rows], state.astype(_BF16)) * dq_ref[...]
        ret_blocks.append(intra + cross)
    ret = ret_blocks[0] if len(ret_blocks) == 1 else jnp.concatenate(ret_blocks, axis=0)

    gn = []
    for h in range(RET_HEADS):
        o_h = ret[:, h * RET_DV:(h + 1) * RET_DV]
        mu = jnp.mean(o_h, axis=-1, keepdims=True)
        d = o_h - mu
        var = jnp.mean(d * d, axis=-1, keepdims=True)
        gn.append(d * lax.rsqrt(var + GN_EPS))
    r = gate * (jnp.concatenate(gn, axis=1) * gain_ref[...])

    mix = (_dot(r.astype(_BF16), wout_ref[0:RET_VWIDTH, :]) +
           _dot(pool, wout_ref[RET_VWIDTH:, :]))
    o_ref[...] = x + mix


def _mixer_call(x2d, batch, seq, gain, w_in, w_out, gn_gain, pool_w, pool_scale):
    tm = MIX_TILE
    steps = seq // tm
    tables = _mixer_tables(seq)
    row_spec = pl.BlockSpec((tm, D_MODEL), lambda b, s: (b * steps + s, 0))
    small = [gain.reshape(1, D_MODEL), w_in, w_out, gn_gain.reshape(1, RET_VWIDTH),
             pool_w, pool_scale.reshape(1, POOL_WIDTH), *tables]
    in_specs = [row_spec] + [_resident(a.shape) for a in small]
    resident_bytes = sum(int(np.prod(a.shape)) * a.dtype.itemsize for a in small)
    temp_bytes = tm * (IN_WIDTH * 4 * 2 + D_MODEL * 4 * 6) + 4 * RET_BLOCK * RET_BLOCK * 8
    return pl.pallas_call(
        functools.partial(_mixer_kernel, tm=tm),
        out_shape=jax.ShapeDtypeStruct(x2d.shape, _F32),
        grid=(batch, steps),
        in_specs=in_specs,
        out_specs=row_spec,
        scratch_shapes=[pltpu.VMEM((RET_QKWIDTH, RET_VWIDTH), _F32),
                        pltpu.VMEM((POOL_HALO + tm, POOL_WIDTH), _F32)],
        compiler_params=pltpu.CompilerParams(
            dimension_semantics=("arbitrary", "arbitrary"),
            vmem_limit_bytes=resident_bytes + 4 * tm * D_MODEL * 4 + temp_bytes + 8 * _MIB),
        name="mixer",
    )(x2d, *small)


def kernel(x, norm_ffn1, ffn1_gate, ffn1_up, ffn1_down, norm_mix, w_in, ret_gn_gain, pool_w,
           pool_scale, w_out, norm_ffn2, ffn2_gate, ffn2_up, ffn2_down, norm_final):
    batch, seq, d_model = x.shape
    assert d_model == D_MODEL and seq % MIX_TILE == 0 and (batch * seq) % FFN_TILE == 0
    bf = lambda a: a.astype(_BF16)
    h = x.reshape(batch * seq, d_model)
    depth = norm_ffn1.shape[0]
    for l in range(depth):
        last = l == depth - 1
        h = _ffn_call(h, norm_ffn1[l], bf(ffn1_gate[l]), bf(ffn1_up[l]), bf(ffn1_down[l]))
        h = _mixer_call(h, batch, seq, norm_mix[l], bf(w_in[l]), bf(w_out[l]), ret_gn_gain[l],
                        bf(pool_w[l]), pool_scale[l])
        h = _ffn_call(h, norm_ffn2[l], bf(ffn2_gate[l]), bf(ffn2_up[l]), bf(ffn2_down[l]),
                      final_gain=norm_final if last else None)
    return h.reshape(batch, seq, d_model)
```

```python
import functools

import numpy as np
import jax
import jax.numpy as jnp
from jax import lax
from jax.experimental import pallas as pl
from jax.experimental.pallas import tpu as pltpu

D_MODEL = 1024
D_FF = 2816
CHUNK = 64
RET_HEADS = 4
RET_DK = 64
RET_DV = 128
RET_QKWIDTH = RET_HEADS * RET_DK
RET_VWIDTH = RET_HEADS * RET_DV
POOL_WINDOWS = (2, 4, 8, 16)
POOL_GC = 128
POOL_WIDTH = len(POOL_WINDOWS) * POOL_GC
POOL_HALO = 16
IN_WIDTH = 2 * RET_QKWIDTH + 2 * RET_VWIDTH + POOL_WIDTH
ROPE_BASE = 10000.0
RMS_EPS = 1e-6
GN_EPS = 1e-5

RET_BLOCK = 256
FFN_TILE = 1024
FFN_SUB = 512
FF_CHUNK = 256
MIX_TILE = 1024
WEIGHT_CHUNKS = 16
WEIGHT_RING = 4

_F32 = jnp.float32
_BF16 = jnp.bfloat16
_MIB = 1024 * 1024


def _rms_scale(x):
    return lax.rsqrt(jnp.mean(x * x, axis=-1, keepdims=True) + RMS_EPS)


def _rms_norm(x, gain):
    return x * _rms_scale(x) * gain


def _dot(a, b):
    return jnp.dot(a, b, preferred_element_type=_F32)


def _chunk_copy(pairs, stage, sem, j):
    ring, chunk_rows = stage.shape[0], stage.shape[1]
    chunks_per_weight = pairs[0][0].shape[0] // chunk_rows
    src, c = pairs[j // chunks_per_weight][0], j % chunks_per_weight
    return pltpu.make_async_copy(src.at[pl.ds(c * chunk_rows, chunk_rows), :],
                                 stage.at[j % ring], sem.at[j % ring])


def _load_weights_bf16(groups):
    for pairs, stage, sem in groups:
        for j in range(WEIGHT_RING - 1):
            _chunk_copy(pairs, stage, sem, j).start()
    for pairs, stage, sem in groups:
        chunk_rows = stage.shape[1]
        n_chunks = WEIGHT_CHUNKS * len(pairs)
        for j in range(n_chunks):
            if j + WEIGHT_RING - 1 < n_chunks:
                _chunk_copy(pairs, stage, sem, j + WEIGHT_RING - 1).start()
            _chunk_copy(pairs, stage, sem, j).wait()
            dst, c = pairs[j // WEIGHT_CHUNKS][1], j % WEIGHT_CHUNKS
            dst[c * chunk_rows:(c + 1) * chunk_rows, :] = stage[j % WEIGHT_RING].astype(_BF16)


def _weight_scratch(shapes):
    homes = [pltpu.VMEM(s, _BF16) for s in shapes]
    rings = []
    for s in dict.fromkeys(shapes):
        rings += [pltpu.VMEM((WEIGHT_RING, s[0] // WEIGHT_CHUNKS, s[1]), _F32),
                  pltpu.SemaphoreType.DMA((WEIGHT_RING,))]
    return homes + rings


def _weight_scratch_bytes(shapes):
    homes = sum(2 * s[0] * s[1] for s in shapes)
    rings = sum(WEIGHT_RING * 4 * (s[0] // WEIGHT_CHUNKS) * s[1] for s in dict.fromkeys(shapes))
    return homes + rings


def _ffn_kernel(x_ref, g_ref, wg_hbm, wu_hbm, wd_hbm, *rest, final_norm):
    if final_norm:
        gf_ref, *rest = rest
    o_ref, wg_ref, wu_ref, wd_ref, up_stage, up_sem, down_stage, down_sem = rest

    @pl.when(pl.program_id(0) == 0)
    def _():
        _load_weights_bf16([([(wg_hbm, wg_ref), (wu_hbm, wu_ref)], up_stage, up_sem),
                            ([(wd_hbm, wd_ref)], down_stage, down_sem)])

    for sub in range(x_ref.shape[0] // FFN_SUB):
        rows = slice(sub * FFN_SUB, (sub + 1) * FFN_SUB)
        x = x_ref[rows, :]
        xg = (x * g_ref[...]).astype(_BF16)
        r = jnp.broadcast_to(_rms_scale(x), (FFN_SUB, FF_CHUNK))
        hidden = []
        for c in range(D_FF // FF_CHUNK):
            cols = slice(c * FF_CHUNK, (c + 1) * FF_CHUNK)
            gate = _dot(xg, wg_ref[:, cols]) * r
            up = _dot(xg, wu_ref[:, cols]) * r
            hidden.append((gate * jax.nn.sigmoid(gate) * up).astype(_BF16))
        h = jnp.concatenate(hidden, axis=1)
        out = x + 0.5 * _dot(h, wd_ref[...])
        if final_norm:
            out = _rms_norm(out, gf_ref[...])
        o_ref[rows, :] = out


def _resident(shape):
    zeros = (0,) * len(shape)
    return pl.BlockSpec(shape, lambda *_: zeros, pipeline_mode=pl.Buffered(1))


_IN_HBM = pl.BlockSpec(memory_space=pl.ANY)


def _ffn_call(x2d, gain, wg, wu, wd, final_gain=None):
    rows = x2d.shape[0]
    tm = FFN_TILE
    final_norm = final_gain is not None
    row_spec = pl.BlockSpec((tm, D_MODEL), lambda i: (i, 0))
    in_specs = [row_spec, _resident((1, D_MODEL)), _IN_HBM, _IN_HBM, _IN_HBM]
    args = [x2d, gain.reshape(1, D_MODEL), wg, wu, wd]
    if final_norm:
        in_specs.append(_resident((1, D_MODEL)))
        args.append(final_gain.reshape(1, D_MODEL))
    weight_shapes = [wg.shape, wu.shape, wd.shape]
    io_bytes = 4 * tm * D_MODEL * 4
    temp_bytes = tm * D_FF * 2 * 2 + tm * D_MODEL * (2 + 4 + 4) + 8 * FFN_SUB * FF_CHUNK * 4
    return pl.pallas_call(
        functools.partial(_ffn_kernel, final_norm=final_norm),
        out_shape=jax.ShapeDtypeStruct((rows, D_MODEL), _F32),
        grid=(rows // tm,),
        in_specs=in_specs,
        out_specs=row_spec,
        scratch_shapes=_weight_scratch(weight_shapes),
        compiler_params=pltpu.CompilerParams(
            dimension_semantics=("arbitrary",),
            vmem_limit_bytes=(_weight_scratch_bytes(weight_shapes) + io_bytes + temp_bytes +
                              4 * _MIB)),
        name="ffn_final" if final_norm else "ffn",
    )(*args)


def _mixer_tables(seq):
    heads = np.arange(RET_HEADS)
    gamma = 1.0 - 2.0 ** (-5.0 - heads)
    i = np.arange(RET_BLOCK)
    diff = i[:, None] - i[None, :]
    same = (i[:, None] // CHUNK) == (i[None, :] // CHUNK)
    earlier = (i[None, :] // CHUNK) < (i[:, None] // CHUNK)
    expo = np.where(same, np.abs(diff), diff)
    q_scale = RET_DK ** -0.5
    dmat = np.where((same | earlier)[None], gamma[:, None, None] ** expo[None], 0.0) * q_scale
    dq = np.repeat((gamma[None, :] ** (i[:, None] + 1.0)) * q_scale, RET_DV, axis=1)
    dk = np.repeat(gamma[None, :] ** (RET_BLOCK - 1.0 - i[:, None]), RET_DK, axis=1)
    dc = np.repeat(gamma ** float(RET_BLOCK), RET_DV)[None, :]
    bm = (np.arange(RET_QKWIDTH)[:, None] // RET_DK ==
          np.arange(RET_VWIDTH)[None, :] // RET_DV).astype(np.float64)
    half = RET_DK // 2
    freqs = ROPE_BASE ** (-np.arange(half) * 2.0 / RET_DK)
    ang = np.arange(seq)[:, None] * freqs[None, :]
    cos = np.tile(np.cos(ang), (1, 4))
    sin = np.tile(np.concatenate([-np.sin(ang), np.sin(ang)], axis=1), (1, 2))
    f32 = lambda a: jnp.asarray(a, dtype=_F32)
    return tuple(f32(a) for a in (cos, sin, dmat, dq, dk, dc, bm))


def _rotary(t, cos, sin):
    lane = lax.broadcasted_iota(jnp.int32, (t.shape[0], 128), 1)
    first_half = (lane & (RET_DK - 1)) < (RET_DK // 2)
    cols = []
    for c in range(t.shape[1] // 128):
        tc = t[:, c * 128:(c + 1) * 128]
        partner = jnp.where(first_half, pltpu.roll(tc, 128 - RET_DK // 2, 1),
                            pltpu.roll(tc, RET_DK // 2, 1))
        cols.append(tc * cos + partner * sin)
    return jnp.concatenate(cols, axis=1)


def _mixer_kernel(x_ref, g_ref, win_hbm, wout_hbm, gain_ref, pw_ref, ps_ref,
                  cos_ref, sin_ref, dmat_ref, dq_ref, dk_ref, dc_ref, bm_ref,
                  o_ref, state_ref, u_ref, win_ref, wout_ref, in_stage, in_sem, out_stage, out_sem,
                  *, tm):
    s = pl.program_id(1)
    o1, o2, o3, o4 = RET_QKWIDTH, 2 * RET_QKWIDTH, 2 * RET_QKWIDTH + RET_VWIDTH, IN_WIDTH - POOL_WIDTH

    @pl.when((pl.program_id(0) == 0) & (s == 0))
    def _():
        _load_weights_bf16([([(win_hbm, win_ref)], in_stage, in_sem),
                            ([(wout_hbm, wout_ref)], out_stage, out_sem)])

    @pl.when(s == 0)
    def _():
        state_ref[...] = jnp.zeros_like(state_ref)
        u_ref[0:POOL_HALO, :] = jnp.zeros((POOL_HALO, POOL_WIDTH), _F32)

    @pl.when(s > 0)
    def _():
        u_ref[0:POOL_HALO, :] = u_ref[tm:tm + POOL_HALO, :]

    x = x_ref[...]
    n = (x * g_ref[...]).astype(_BF16)
    r_in = jnp.broadcast_to(_rms_scale(x), (tm, POOL_WIDTH))
    row0 = pl.multiple_of(s * tm, tm)

    u = _dot(n, win_ref[:, o4:]) * r_in
    u_ref[POOL_HALO:POOL_HALO + tm, :] = u
    pos = row0 + lax.broadcasted_iota(jnp.int32, (tm, 1), 0)
    pooled = []
    for gi, w in enumerate(POOL_WINDOWS):
        cs = slice(gi * POOL_GC, (gi + 1) * POOL_GC)
        acc = u_ref[:, cs]
        shift = 1
        while shift < w:
            acc = acc + pltpu.roll(acc, shift, 0)
            shift *= 2
        cnt = jnp.minimum(pos + 1, w).astype(_F32)
        pooled.append((acc[POOL_HALO:, :] / cnt - u[:, cs]).astype(_BF16))

    gate = _dot(n, win_ref[:, o3:o4]) * r_in
    gate = gate * jax.nn.sigmoid(gate)
    qkv = _dot(n, win_ref[:, :o3]) * jnp.concatenate([r_in, r_in], axis=1)
    pool = jnp.concatenate([_dot(pooled[gi], pw_ref[gi]) for gi in range(len(POOL_WINDOWS))],
                           axis=1)
    pool = (pool * ps_ref[...]).astype(_BF16)

    cos = cos_ref[pl.ds(row0, tm), :]
    sin = sin_ref[pl.ds(row0, tm), :]
    q = _rotary(qkv[:, :o1], cos, sin).astype(_BF16)
    k = _rotary(qkv[:, o1:o2], cos, sin)
    v = qkv[:, o2:o3].astype(_BF16)

    lane_head = lax.broadcasted_iota(jnp.int32, (RET_BLOCK, RET_QKWIDTH), 1) // RET_DK
    blocks = [slice(b * RET_BLOCK, (b + 1) * RET_BLOCK) for b in range(tm // RET_BLOCK)]
    states = [state_ref[...]]
    intra_blocks = []
    for rows in blocks:
        qb, kb, vb = q[rows], k[rows], v[rows]
        decayed = []
        for h in range(RET_HEADS):
            kh = jnp.where(lane_head == h, kb, 0.0).astype(_BF16)
            sc = lax.dot_general(qb, kh, (((1,), (1,)), ((), ())),
                                 preferred_element_type=_F32)
            decayed.append((sc * dmat_ref[h]).astype(_BF16))
        kd = (kb * dk_ref[...]).astype(_BF16)
        kv = lax.dot_general(kd, vb, (((0,), (0,)), ((), ())),
                             preferred_element_type=_F32)
        states.append(states[-1] * dc_ref[...] + kv * bm_ref[...])
        intra_blocks.append(jnp.concatenate(
            [_dot(decayed[h], vb[:, h * RET_DV:(h + 1) * RET_DV]) for h in range(RET_HEADS)],
            axis=1))
    state_ref[...] = states[-1]
    ret_blocks = []
    for rows, state, intra in zip(blocks, states, intra_blocks):
        cross = _dot(q[rows], state.astype(_BF16)) * dq_ref[...]
        ret_blocks.append(intra + cross)
    ret = ret_blocks[0] if len(ret_blocks) == 1 else jnp.concatenate(ret_blocks, axis=0)

    gn = []
    for h in range(RET_HEADS):
        o_h = ret[:, h * RET_DV:(h + 1) * RET_DV]
        mu = jnp.mean(o_h, axis=-1, keepdims=True)
        d = o_h - mu
        var = jnp.mean(d * d, axis=-1, keepdims=True)
        gn.append(d * lax.rsqrt(var + GN_EPS))
    r = gate * (jnp.concatenate(gn, axis=1) * gain_ref[...])

    mix = (_dot(r.astype(_BF16), wout_ref[0:RET_VWIDTH, :]) +
           _dot(pool, wout_ref[RET_VWIDTH:, :]))
    o_ref[...] = x + mix


def _mixer_call(x2d, batch, seq, gain, w_in, w_out, gn_gain, pool_w, pool_scale):
    tm = MIX_TILE
    steps = seq // tm
    row_spec = pl.BlockSpec((tm, D_MODEL), lambda b, s: (b * steps + s, 0))
    small = [gn_gain.reshape(1, RET_VWIDTH), pool_w.astype(_BF16),
             pool_scale.reshape(1, POOL_WIDTH), *_mixer_tables(seq)]
    in_specs = ([row_spec, _resident((1, D_MODEL)), _IN_HBM, _IN_HBM] +
                [_resident(a.shape) for a in small])
    weight_shapes = [w_in.shape, w_out.shape]
    resident_bytes = (sum(int(np.prod(a.shape)) * a.dtype.itemsize for a in small) +
                      _weight_scratch_bytes(weight_shapes))
    temp_bytes = tm * (IN_WIDTH * 4 * 2 + D_MODEL * 4 * 6) + 4 * RET_BLOCK * RET_BLOCK * 8
    return pl.pallas_call(
        functools.partial(_mixer_kernel, tm=tm),
        out_shape=jax.ShapeDtypeStruct(x2d.shape, _F32),
        grid=(batch, steps),
        in_specs=in_specs,
        out_specs=row_spec,
        scratch_shapes=[pltpu.VMEM((RET_QKWIDTH, RET_VWIDTH), _F32),
                        pltpu.VMEM((POOL_HALO + tm, POOL_WIDTH), _F32),
                        *_weight_scratch(weight_shapes)],
        compiler_params=pltpu.CompilerParams(
            dimension_semantics=("arbitrary", "arbitrary"),
            vmem_limit_bytes=resident_bytes + 4 * tm * D_MODEL * 4 + temp_bytes + 8 * _MIB),
        name="mixer",
    )(x2d, gain.reshape(1, D_MODEL), w_in, w_out, *small)


def kernel(x, norm_ffn1, ffn1_gate, ffn1_up, ffn1_down, norm_mix, w_in, ret_gn_gain, pool_w,
           pool_scale, w_out, norm_ffn2, ffn2_gate, ffn2_up, ffn2_down, norm_final):
    batch, seq, d_model = x.shape
    assert d_model == D_MODEL and seq % MIX_TILE == 0 and (batch * seq) % FFN_TILE == 0
    h = x.reshape(batch * seq, d_model)
    depth = norm_ffn1.shape[0]
    for l in range(depth):
        last = l == depth - 1
        h = _ffn_call(h, norm_ffn1[l], ffn1_gate[l], ffn1_up[l], ffn1_down[l])
        h = _mixer_call(h, batch, seq, norm_mix[l], w_in[l], w_out[l], ret_gn_gain[l],
                        pool_w[l], pool_scale[l])
        h = _ffn_call(h, norm_ffn2[l], ffn2_gate[l], ffn2_up[l], ffn2_down[l],
                      final_gain=norm_final if last else None)
    return h.reshape(batch, seq, d_model)
```

```python
import functools

import numpy as np
import jax
import jax.numpy as jnp
from jax import lax
from jax.experimental import pallas as pl
from jax.experimental.pallas import tpu as pltpu

D_MODEL = 1024
D_FF = 2816
CHUNK = 64
RET_HEADS = 4
RET_DK = 64
RET_DV = 128
RET_QKWIDTH = RET_HEADS * RET_DK
RET_VWIDTH = RET_HEADS * RET_DV
POOL_WINDOWS = (2, 4, 8, 16)
POOL_GC = 128
POOL_WIDTH = len(POOL_WINDOWS) * POOL_GC
POOL_HALO = 16
IN_WIDTH = 2 * RET_QKWIDTH + 2 * RET_VWIDTH + POOL_WIDTH
ROPE_BASE = 10000.0
RMS_EPS = 1e-6
GN_EPS = 1e-5

RET_BLOCK = 256
FFN_TILE = 2048
FFN_SUB = 512
FF_CHUNK = 256
MIX_TILE = 1024

_F32 = jnp.float32
_BF16 = jnp.bfloat16
_MIB = 1024 * 1024
V7X_VMEM_BYTES = 64 * _MIB


def _rms_scale(x):
    return lax.rsqrt(jnp.mean(x * x, axis=-1, keepdims=True) + RMS_EPS)


def _rms_norm(x, gain):
    return x * _rms_scale(x) * gain


def _dot(a, b):
    return jnp.dot(a, b, preferred_element_type=_F32)


def _ffn_kernel(x_ref, g_ref, wg_ref, wu_ref, wd_ref, *rest, final_norm):
    if final_norm:
        gf_ref, o_ref = rest
    else:
        (o_ref,) = rest
    for sub in range(x_ref.shape[0] // FFN_SUB):
        rows = slice(sub * FFN_SUB, (sub + 1) * FFN_SUB)
        x = x_ref[rows, :]
        xg = (x * g_ref[...]).astype(_BF16)
        r = jnp.broadcast_to(_rms_scale(x), (FFN_SUB, FF_CHUNK))
        hidden = []
        for c in range(D_FF // FF_CHUNK):
            cols = slice(c * FF_CHUNK, (c + 1) * FF_CHUNK)
            gate = _dot(xg, wg_ref[:, cols]) * r
            up = _dot(xg, wu_ref[:, cols]) * r
            hidden.append((gate * jax.nn.sigmoid(gate) * up).astype(_BF16))
        h = jnp.concatenate(hidden, axis=1)
        out = x + 0.5 * _dot(h, wd_ref[...])
        if final_norm:
            out = _rms_norm(out, gf_ref[...])
        o_ref[rows, :] = out


def _resident(shape):
    zeros = (0,) * len(shape)
    return pl.BlockSpec(shape, lambda *_: zeros, pipeline_mode=pl.Buffered(1))


def _ffn_call(x2d, gain, wg, wu, wd, final_gain=None):
    rows = x2d.shape[0]
    tm = FFN_TILE
    final_norm = final_gain is not None
    row_spec = pl.BlockSpec((tm, D_MODEL), lambda i: (i, 0))
    in_specs = [row_spec, _resident((1, D_MODEL)), _resident((D_MODEL, D_FF)),
                _resident((D_MODEL, D_FF)), _resident((D_FF, D_MODEL))]
    args = [x2d, gain.reshape(1, D_MODEL), wg, wu, wd]
    if final_norm:
        in_specs.append(_resident((1, D_MODEL)))
        args.append(final_gain.reshape(1, D_MODEL))
    weight_bytes = 3 * D_MODEL * D_FF * 2
    io_bytes = 4 * tm * D_MODEL * 4
    temp_bytes = 2 * FFN_SUB * (D_FF * 2 + D_MODEL * (2 + 4)) + 8 * FFN_SUB * FF_CHUNK * 4
    return pl.pallas_call(
        functools.partial(_ffn_kernel, final_norm=final_norm),
        out_shape=jax.ShapeDtypeStruct((rows, D_MODEL), _F32),
        grid=(rows // tm,),
        in_specs=in_specs,
        out_specs=row_spec,
        compiler_params=pltpu.CompilerParams(
            dimension_semantics=("arbitrary",),
            vmem_limit_bytes=min(weight_bytes + io_bytes + temp_bytes + 4 * _MIB,
                                 V7X_VMEM_BYTES)),
        name="ffn_final" if final_norm else "ffn",
    )(*args)


def _mixer_tables(seq):
    heads = np.arange(RET_HEADS)
    gamma = 1.0 - 2.0 ** (-5.0 - heads)
    i = np.arange(RET_BLOCK)
    diff = i[:, None] - i[None, :]
    same = (i[:, None] // CHUNK) == (i[None, :] // CHUNK)
    earlier = (i[None, :] // CHUNK) < (i[:, None] // CHUNK)
    expo = np.where(same, np.abs(diff), diff)
    q_scale = RET_DK ** -0.5
    dmat = np.where((same | earlier)[None], gamma[:, None, None] ** expo[None], 0.0) * q_scale
    dq = np.repeat((gamma[None, :] ** (i[:, None] + 1.0)) * q_scale, RET_DV, axis=1)
    dk = np.repeat(gamma[None, :] ** (RET_BLOCK - 1.0 - i[:, None]), RET_DK, axis=1)
    dc = np.repeat(gamma ** float(RET_BLOCK), RET_DV)[None, :]
    bm = (np.arange(RET_QKWIDTH)[:, None] // RET_DK ==
          np.arange(RET_VWIDTH)[None, :] // RET_DV).astype(np.float64)
    half = RET_DK // 2
    freqs = ROPE_BASE ** (-np.arange(half) * 2.0 / RET_DK)
    ang = np.arange(seq)[:, None] * freqs[None, :]
    cos = np.tile(np.cos(ang), (1, 4))
    sin = np.tile(np.concatenate([-np.sin(ang), np.sin(ang)], axis=1), (1, 2))
    f32 = lambda a: jnp.asarray(a, dtype=_F32)
    return tuple(f32(a) for a in (cos, sin, dmat, dq, dk, dc, bm))


def _rotary(t, cos, sin):
    lane = lax.broadcasted_iota(jnp.int32, (t.shape[0], 128), 1)
    first_half = (lane & (RET_DK - 1)) < (RET_DK // 2)
    cols = []
    for c in range(t.shape[1] // 128):
        tc = t[:, c * 128:(c + 1) * 128]
        partner = jnp.where(first_half, pltpu.roll(tc, 128 - RET_DK // 2, 1),
                            pltpu.roll(tc, RET_DK // 2, 1))
        cols.append(tc * cos + partner * sin)
    return jnp.concatenate(cols, axis=1)


def _mixer_kernel(x_ref, g_ref, win_ref, wout_ref, gain_ref, pw_ref, ps_ref,
                  cos_ref, sin_ref, dmat_ref, dq_ref, dk_ref, dc_ref, bm_ref,
                  o_ref, state_ref, u_ref, *, tm):
    s = pl.program_id(1)
    o1, o2, o3, o4 = RET_QKWIDTH, 2 * RET_QKWIDTH, 2 * RET_QKWIDTH + RET_VWIDTH, IN_WIDTH - POOL_WIDTH

    @pl.when(s == 0)
    def _():
        state_ref[...] = jnp.zeros_like(state_ref)
        u_ref[0:POOL_HALO, :] = jnp.zeros((POOL_HALO, POOL_WIDTH), _F32)

    @pl.when(s > 0)
    def _():
        u_ref[0:POOL_HALO, :] = u_ref[tm:tm + POOL_HALO, :]

    x = x_ref[...]
    n = (x * g_ref[...]).astype(_BF16)
    r_in = jnp.broadcast_to(_rms_scale(x), (tm, POOL_WIDTH))
    row0 = pl.multiple_of(s * tm, tm)

    u = _dot(n, win_ref[:, o4:]) * r_in
    u_ref[POOL_HALO:POOL_HALO + tm, :] = u
    pos = row0 + lax.broadcasted_iota(jnp.int32, (tm, 1), 0)
    pooled = []
    for gi, w in enumerate(POOL_WINDOWS):
        cs = slice(gi * POOL_GC, (gi + 1) * POOL_GC)
        acc = u_ref[:, cs]
        shift = 1
        while shift < w:
            acc = acc + pltpu.roll(acc, shift, 0)
            shift *= 2
        cnt = jnp.minimum(pos + 1, w).astype(_F32)
        pooled.append((acc[POOL_HALO:, :] / cnt - u[:, cs]).astype(_BF16))

    gate = _dot(n, win_ref[:, o3:o4]) * r_in
    gate = gate * jax.nn.sigmoid(gate)
    qkv = _dot(n, win_ref[:, :o3]) * jnp.concatenate([r_in, r_in], axis=1)
    pool = jnp.concatenate([_dot(pooled[gi], pw_ref[gi]) for gi in range(len(POOL_WINDOWS))],
                           axis=1)
    pool = (pool * ps_ref[...]).astype(_BF16)

    cos = cos_ref[pl.ds(row0, tm), :]
    sin = sin_ref[pl.ds(row0, tm), :]
    q = _rotary(qkv[:, :o1], cos, sin).astype(_BF16)
    k = _rotary(qkv[:, o1:o2], cos, sin)
    v = qkv[:, o2:o3].astype(_BF16)

    lane_head = lax.broadcasted_iota(jnp.int32, (RET_BLOCK, RET_QKWIDTH), 1) // RET_DK
    blocks = [slice(b * RET_BLOCK, (b + 1) * RET_BLOCK) for b in range(tm // RET_BLOCK)]
    states = [state_ref[...]]
    intra_blocks = []
    for rows in blocks:
        qb, kb, vb = q[rows], k[rows], v[rows]
        decayed = []
        for h in range(RET_HEADS):
            kh = jnp.where(lane_head == h, kb, 0.0).astype(_BF16)
            sc = lax.dot_general(qb, kh, (((1,), (1,)), ((), ())),
                                 preferred_element_type=_F32)
            decayed.append((sc * dmat_ref[h]).astype(_BF16))
        kd = (kb * dk_ref[...]).astype(_BF16)
        kv = lax.dot_general(kd, vb, (((0,), (0,)), ((), ())),
                             preferred_element_type=_F32)
        states.append(states[-1] * dc_ref[...] + kv * bm_ref[...])
        intra_blocks.append(jnp.concatenate(
            [_dot(decayed[h], vb[:, h * RET_DV:(h + 1) * RET_DV]) for h in range(RET_HEADS)],
            axis=1))
    state_ref[...] = states[-1]
    ret_blocks = []
    for rows, state, intra in zip(blocks, states, intra_blocks):
        cross = _dot(q[rows], state.astype(_BF16)) * dq_ref[...]
        ret_blocks.append(intra + cross)
    ret = ret_blocks[0] if len(ret_blocks) == 1 else jnp.concatenate(ret_blocks, axis=0)

    gn = []
    for h in range(RET_HEADS):
        o_h = ret[:, h * RET_DV:(h + 1) * RET_DV]
        mu = jnp.mean(o_h, axis=-1, keepdims=True)
        d = o_h - mu
        var = jnp.mean(d * d, axis=-1, keepdims=True)
        gn.append(d * lax.rsqrt(var + GN_EPS))
    r = gate * (jnp.concatenate(gn, axis=1) * gain_ref[...])

    mix = (_dot(r.astype(_BF16), wout_ref[0:RET_VWIDTH, :]) +
           _dot(pool, wout_ref[RET_VWIDTH:, :]))
    o_ref[...] = x + mix


def _mixer_call(x2d, batch, seq, gain, w_in, w_out, gn_gain, pool_w, pool_scale):
    tm = MIX_TILE
    steps = seq // tm
    tables = _mixer_tables(seq)
    row_spec = pl.BlockSpec((tm, D_MODEL), lambda b, s: (b * steps + s, 0))
    small = [gain.reshape(1, D_MODEL), w_in, w_out, gn_gain.reshape(1, RET_VWIDTH),
             pool_w, pool_scale.reshape(1, POOL_WIDTH), *tables]
    in_specs = [row_spec] + [_resident(a.shape) for a in small]
    resident_bytes = sum(int(np.prod(a.shape)) * a.dtype.itemsize for a in small)
    temp_bytes = tm * (IN_WIDTH * 4 * 2 + D_MODEL * 4 * 6) + 4 * RET_BLOCK * RET_BLOCK * 8
    return pl.pallas_call(
        functools.partial(_mixer_kernel, tm=tm),
        out_shape=jax.ShapeDtypeStruct(x2d.shape, _F32),
        grid=(batch, steps),
        in_specs=in_specs,
        out_specs=row_spec,
        scratch_shapes=[pltpu.VMEM((RET_QKWIDTH, RET_VWIDTH), _F32),
                        pltpu.VMEM((POOL_HALO + tm, POOL_WIDTH), _F32)],
        compiler_params=pltpu.CompilerParams(
            dimension_semantics=("arbitrary", "arbitrary"),
            vmem_limit_bytes=resident_bytes + 4 * tm * D_MODEL * 4 + temp_bytes + 8 * _MIB),
        name="mixer",
    )(x2d, *small)


def kernel(x, norm_ffn1, ffn1_gate, ffn1_up, ffn1_down, norm_mix, w_in, ret_gn_gain, pool_w,
           pool_scale, w_out, norm_ffn2, ffn2_gate, ffn2_up, ffn2_down, norm_final):
    batch, seq, d_model = x.shape
    assert d_model == D_MODEL and seq % MIX_TILE == 0 and (batch * seq) % FFN_TILE == 0
    bf = lambda a: a.astype(_BF16)
    h = x.reshape(batch * seq, d_model)
    depth = norm_ffn1.shape[0]
    for l in range(depth):
        last = l == depth - 1
        h = _ffn_call(h, norm_ffn1[l], bf(ffn1_gate[l]), bf(ffn1_up[l]), bf(ffn1_down[l]))
        h = _mixer_call(h, batch, seq, norm_mix[l], bf(w_in[l]), bf(w_out[l]), ret_gn_gain[l],
                        bf(pool_w[l]), pool_scale[l])
        h = _ffn_call(h, norm_ffn2[l], bf(ffn2_gate[l]), bf(ffn2_up[l]), bf(ffn2_down[l]),
                      final_gain=norm_final if last else None)
    return h.reshape(batch, seq, d_model)
```

```python
import functools

import numpy as np
import jax
import jax.numpy as jnp
from jax import lax
from jax.experimental import pallas as pl
from jax.experimental.pallas import tpu as pltpu

D_MODEL = 1024
D_FF = 2816
CHUNK = 64
RET_HEADS = 4
RET_DK = 64
RET_DV = 128
RET_QKWIDTH = RET_HEADS * RET_DK
RET_VWIDTH = RET_HEADS * RET_DV
POOL_WINDOWS = (2, 4, 8, 16)
POOL_GC = 128
POOL_WIDTH = len(POOL_WINDOWS) * POOL_GC
POOL_HALO = 16
IN_WIDTH = 2 * RET_QKWIDTH + 2 * RET_VWIDTH + POOL_WIDTH
ROPE_BASE = 10000.0
RMS_EPS = 1e-6
GN_EPS = 1e-5

RET_BLOCK = 256
FFN_TILE = 1024
FFN_SUB = 512
FF_CHUNK = 256
MIX_TILE = 1024
WEIGHT_CHUNKS = 16
WEIGHT_RING = 4
DMA_PRIORITIES = 2

_F32 = jnp.float32
_BF16 = jnp.bfloat16
_MIB = 1024 * 1024


def _rms_scale(x):
    return lax.rsqrt(jnp.mean(x * x, axis=-1, keepdims=True) + RMS_EPS)


def _rms_norm(x, gain):
    return x * _rms_scale(x) * gain


def _dot(a, b):
    return jnp.dot(a, b, preferred_element_type=_F32)


def _chunk_copy(pairs, stage, sem, j):
    ring, chunk_rows = stage.shape[0], stage.shape[1]
    chunks_per_weight = pairs[0][0].shape[0] // chunk_rows
    src, c = pairs[j // chunks_per_weight][0], j % chunks_per_weight
    return pltpu.make_async_copy(src.at[pl.ds(c * chunk_rows, chunk_rows), :],
                                 stage.at[j % ring], sem.at[j % ring])


def _load_weights_bf16(groups):
    for pairs, stage, sem in groups:
        for j in range(WEIGHT_RING - 1):
            _chunk_copy(pairs, stage, sem, j).start(priority=j % DMA_PRIORITIES)
    for pairs, stage, sem in groups:
        chunk_rows = stage.shape[1]
        n_chunks = WEIGHT_CHUNKS * len(pairs)
        for j in range(n_chunks):
            nxt = j + WEIGHT_RING - 1
            if nxt < n_chunks:
                _chunk_copy(pairs, stage, sem, nxt).start(priority=nxt % DMA_PRIORITIES)
            _chunk_copy(pairs, stage, sem, j).wait()
            dst, c = pairs[j // WEIGHT_CHUNKS][1], j % WEIGHT_CHUNKS
            dst[c * chunk_rows:(c + 1) * chunk_rows, :] = stage[j % WEIGHT_RING].astype(_BF16)


def _weight_scratch(shapes):
    homes = [pltpu.VMEM(s, _BF16) for s in shapes]
    rings = []
    for s in dict.fromkeys(shapes):
        rings += [pltpu.VMEM((WEIGHT_RING, s[0] // WEIGHT_CHUNKS, s[1]), _F32),
                  pltpu.SemaphoreType.DMA((WEIGHT_RING,))]
    return homes + rings


def _weight_scratch_bytes(shapes):
    homes = sum(2 * s[0] * s[1] for s in shapes)
    rings = sum(WEIGHT_RING * 4 * (s[0] // WEIGHT_CHUNKS) * s[1] for s in dict.fromkeys(shapes))
    return homes + rings


def _ffn_kernel(x_ref, g_ref, wg_hbm, wu_hbm, wd_hbm, *rest, final_norm):
    if final_norm:
        gf_ref, *rest = rest
    o_ref, wg_ref, wu_ref, wd_ref, up_stage, up_sem, down_stage, down_sem = rest

    @pl.when(pl.program_id(0) == 0)
    def _():
        _load_weights_bf16([([(wg_hbm, wg_ref), (wu_hbm, wu_ref)], up_stage, up_sem),
                            ([(wd_hbm, wd_ref)], down_stage, down_sem)])

    for sub in range(x_ref.shape[0] // FFN_SUB):
        rows = slice(sub * FFN_SUB, (sub + 1) * FFN_SUB)
        x = x_ref[rows, :]
        xg = (x * g_ref[...]).astype(_BF16)
        r = jnp.broadcast_to(_rms_scale(x), (FFN_SUB, FF_CHUNK))
        hidden = []
        for c in range(D_FF // FF_CHUNK):
            cols = slice(c * FF_CHUNK, (c + 1) * FF_CHUNK)
            gate = _dot(xg, wg_ref[:, cols]) * r
            up = _dot(xg, wu_ref[:, cols]) * r
            hidden.append((gate * jax.nn.sigmoid(gate) * up).astype(_BF16))
        h = jnp.concatenate(hidden, axis=1)
        out = x + 0.5 * _dot(h, wd_ref[...])
        if final_norm:
            out = _rms_norm(out, gf_ref[...])
        o_ref[rows, :] = out


def _resident(shape):
    zeros = (0,) * len(shape)
    return pl.BlockSpec(shape, lambda *_: zeros, pipeline_mode=pl.Buffered(1))


_IN_HBM = pl.BlockSpec(memory_space=pl.ANY)


def _ffn_call(x2d, gain, wg, wu, wd, final_gain=None):
    rows = x2d.shape[0]
    tm = FFN_TILE
    final_norm = final_gain is not None
    row_spec = pl.BlockSpec((tm, D_MODEL), lambda i: (i, 0))
    in_specs = [row_spec, _resident((1, D_MODEL)), _IN_HBM, _IN_HBM, _IN_HBM]
    args = [x2d, gain.reshape(1, D_MODEL), wg, wu, wd]
    if final_norm:
        in_specs.append(_resident((1, D_MODEL)))
        args.append(final_gain.reshape(1, D_MODEL))
    weight_shapes = [wg.shape, wu.shape, wd.shape]
    io_bytes = 4 * tm * D_MODEL * 4
    temp_bytes = tm * D_FF * 2 * 2 + tm * D_MODEL * (2 + 4 + 4) + 8 * FFN_SUB * FF_CHUNK * 4
    return pl.pallas_call(
        functools.partial(_ffn_kernel, final_norm=final_norm),
        out_shape=jax.ShapeDtypeStruct((rows, D_MODEL), _F32),
        grid=(rows // tm,),
        in_specs=in_specs,
        out_specs=row_spec,
        scratch_shapes=_weight_scratch(weight_shapes),
        compiler_params=pltpu.CompilerParams(
            dimension_semantics=("arbitrary",),
            vmem_limit_bytes=(_weight_scratch_bytes(weight_shapes) + io_bytes + temp_bytes +
                              4 * _MIB)),
        name="ffn_final" if final_norm else "ffn",
    )(*args)


def _mixer_tables(seq):
    heads = np.arange(RET_HEADS)
    gamma = 1.0 - 2.0 ** (-5.0 - heads)
    i = np.arange(RET_BLOCK)
    diff = i[:, None] - i[None, :]
    same = (i[:, None] // CHUNK) == (i[None, :] // CHUNK)
    earlier = (i[None, :] // CHUNK) < (i[:, None] // CHUNK)
    expo = np.where(same, np.abs(diff), diff)
    q_scale = RET_DK ** -0.5
    dmat = np.where((same | earlier)[None], gamma[:, None, None] ** expo[None], 0.0) * q_scale
    dq = np.repeat((gamma[None, :] ** (i[:, None] + 1.0)) * q_scale, RET_DV, axis=1)
    dk = np.repeat(gamma[None, :] ** (RET_BLOCK - 1.0 - i[:, None]), RET_DK, axis=1)
    dc = np.repeat(gamma ** float(RET_BLOCK), RET_DV)[None, :]
    bm = (np.arange(RET_QKWIDTH)[:, None] // RET_DK ==
          np.arange(RET_VWIDTH)[None, :] // RET_DV).astype(np.float64)
    half = RET_DK // 2
    freqs = ROPE_BASE ** (-np.arange(half) * 2.0 / RET_DK)
    ang = np.arange(seq)[:, None] * freqs[None, :]
    cos = np.tile(np.cos(ang), (1, 4))
    sin = np.tile(np.concatenate([-np.sin(ang), np.sin(ang)], axis=1), (1, 2))
    f32 = lambda a: jnp.asarray(a, dtype=_F32)
    return tuple(f32(a) for a in (cos, sin, dmat, dq, dk, dc, bm))


def _rotary(t, cos, sin):
    lane = lax.broadcasted_iota(jnp.int32, (t.shape[0], 128), 1)
    first_half = (lane & (RET_DK - 1)) < (RET_DK // 2)
    cols = []
    for c in range(t.shape[1] // 128):
        tc = t[:, c * 128:(c + 1) * 128]
        partner = jnp.where(first_half, pltpu.roll(tc, 128 - RET_DK // 2, 1),
                            pltpu.roll(tc, RET_DK // 2, 1))
        cols.append(tc * cos + partner * sin)
    return jnp.concatenate(cols, axis=1)


def _mixer_kernel(x_ref, g_ref, win_hbm, wout_hbm, gain_ref, pw_ref, ps_ref,
                  cos_ref, sin_ref, dmat_ref, dq_ref, dk_ref, dc_ref, bm_ref,
                  o_ref, state_ref, u_ref, win_ref, wout_ref, in_stage, in_sem, out_stage, out_sem,
                  *, tm):
    s = pl.program_id(1)
    o1, o2, o3, o4 = RET_QKWIDTH, 2 * RET_QKWIDTH, 2 * RET_QKWIDTH + RET_VWIDTH, IN_WIDTH - POOL_WIDTH

    @pl.when((pl.program_id(0) == 0) & (s == 0))
    def _():
        _load_weights_bf16([([(win_hbm, win_ref)], in_stage, in_sem),
                            ([(wout_hbm, wout_ref)], out_stage, out_sem)])

    @pl.when(s == 0)
    def _():
        state_ref[...] = jnp.zeros_like(state_ref)
        u_ref[0:POOL_HALO, :] = jnp.zeros((POOL_HALO, POOL_WIDTH), _F32)

    @pl.when(s > 0)
    def _():
        u_ref[0:POOL_HALO, :] = u_ref[tm:tm + POOL_HALO, :]

    x = x_ref[...]
    n = (x * g_ref[...]).astype(_BF16)
    r_in = jnp.broadcast_to(_rms_scale(x), (tm, POOL_WIDTH))
    row0 = pl.multiple_of(s * tm, tm)

    u = _dot(n, win_ref[:, o4:]) * r_in
    u_ref[POOL_HALO:POOL_HALO + tm, :] = u
    pos = row0 + lax.broadcasted_iota(jnp.int32, (tm, 1), 0)
    pooled = []
    for gi, w in enumerate(POOL_WINDOWS):
        cs = slice(gi * POOL_GC, (gi + 1) * POOL_GC)
        acc = u_ref[:, cs]
        shift = 1
        while shift < w:
            acc = acc + pltpu.roll(acc, shift, 0)
            shift *= 2
        cnt = jnp.minimum(pos + 1, w).astype(_F32)
        pooled.append((acc[POOL_HALO:, :] / cnt - u[:, cs]).astype(_BF16))

    gate = _dot(n, win_ref[:, o3:o4]) * r_in
    gate = gate * jax.nn.sigmoid(gate)
    qkv = _dot(n, win_ref[:, :o3]) * jnp.concatenate([r_in, r_in], axis=1)
    pool = jnp.concatenate([_dot(pooled[gi], pw_ref[gi]) for gi in range(len(POOL_WINDOWS))],
                           axis=1)
    pool = (pool * ps_ref[...]).astype(_BF16)

    cos = cos_ref[pl.ds(row0, tm), :]
    sin = sin_ref[pl.ds(row0, tm), :]
    q = _rotary(qkv[:, :o1], cos, sin).astype(_BF16)
    k = _rotary(qkv[:, o1:o2], cos, sin)
    v = qkv[:, o2:o3].astype(_BF16)

    lane_head = lax.broadcasted_iota(jnp.int32, (RET_BLOCK, RET_QKWIDTH), 1) // RET_DK
    blocks = [slice(b * RET_BLOCK, (b + 1) * RET_BLOCK) for b in range(tm // RET_BLOCK)]
    states = [state_ref[...]]
    intra_blocks = []
    for rows in blocks:
        qb, kb, vb = q[rows], k[rows], v[rows]
        decayed = []
        for h in range(RET_HEADS):
            kh = jnp.where(lane_head == h, kb, 0.0).astype(_BF16)
            sc = lax.dot_general(qb, kh, (((1,), (1,)), ((), ())),
                                 preferred_element_type=_F32)
            decayed.append((sc * dmat_ref[h]).astype(_BF16))
        kd = (kb * dk_ref[...]).astype(_BF16)
        kv = lax.dot_general(kd, vb, (((0,), (0,)), ((), ())),
                             preferred_element_type=_F32)
        states.append(states[-1] * dc_ref[...] + kv * bm_ref[...])
        intra_blocks.append(jnp.concatenate(
            [_dot(decayed[h], vb[:, h * RET_DV:(h + 1) * RET_DV]) for h in range(RET_HEADS)],
            axis=1))
    state_ref[...] = states[-1]
    ret_blocks = []
    for rows, state, intra in zip(blocks, states, intra_blocks):
        cross = _dot(q[rows], state.astype(_BF16)) * dq_ref[...]
        ret_blocks.append(intra + cross)
    ret = ret_blocks[0] if len(ret_blocks) == 1 else jnp.concatenate(ret_blocks, axis=0)

    gn = []
    for h in range(RET_HEADS):
        o_h = ret[:, h * RET_DV:(h + 1) * RET_DV]
        mu = jnp.mean(o_h, axis=-1, keepdims=True)
        d = o_h - mu
        var = jnp.mean(d * d, axis=-1, keepdims=True)
        gn.append(d * lax.rsqrt(var + GN_EPS))
    r = gate * (jnp.concatenate(gn, axis=1) * gain_ref[...])

    mix = (_dot(r.astype(_BF16), wout_ref[0:RET_VWIDTH, :]) +
           _dot(pool, wout_ref[RET_VWIDTH:, :]))
    o_ref[...] = x + mix


def _mixer_call(x2d, batch, seq, gain, w_in, w_out, gn_gain, pool_w, pool_scale):
    tm = MIX_TILE
    steps = seq // tm
    row_spec = pl.BlockSpec((tm, D_MODEL), lambda b, s: (b * steps + s, 0))
    small = [gn_gain.reshape(1, RET_VWIDTH), pool_w.astype(_BF16),
             pool_scale.reshape(1, POOL_WIDTH), *_mixer_tables(seq)]
    in_specs = ([row_spec, _resident((1, D_MODEL)), _IN_HBM, _IN_HBM] +
                [_resident(a.shape) for a in small])
    weight_shapes = [w_in.shape, w_out.shape]
    resident_bytes = (sum(int(np.prod(a.shape)) * a.dtype.itemsize for a in small) +
                      _weight_scratch_bytes(weight_shapes))
    temp_bytes = tm * (IN_WIDTH * 4 * 2 + D_MODEL * 4 * 6) + 4 * RET_BLOCK * RET_BLOCK * 8
    return pl.pallas_call(
        functools.partial(_mixer_kernel, tm=tm),
        out_shape=jax.ShapeDtypeStruct(x2d.shape, _F32),
        grid=(batch, steps),
        in_specs=in_specs,
        out_specs=row_spec,
        scratch_shapes=[pltpu.VMEM((RET_QKWIDTH, RET_VWIDTH), _F32),
                        pltpu.VMEM((POOL_HALO + tm, POOL_WIDTH), _F32),
                        *_weight_scratch(weight_shapes)],
        compiler_params=pltpu.CompilerParams(
            dimension_semantics=("arbitrary", "arbitrary"),
            vmem_limit_bytes=resident_bytes + 4 * tm * D_MODEL * 4 + temp_bytes + 8 * _MIB),
        name="mixer",
    )(x2d, gain.reshape(1, D_MODEL), w_in, w_out, *small)


def kernel(x, norm_ffn1, ffn1_gate, ffn1_up, ffn1_down, norm_mix, w_in, ret_gn_gain, pool_w,
           pool_scale, w_out, norm_ffn2, ffn2_gate, ffn2_up, ffn2_down, norm_final):
    batch, seq, d_model = x.shape
    assert d_model == D_MODEL and seq % MIX_TILE == 0 and (batch * seq) % FFN_TILE == 0
    h = x.reshape(batch * seq, d_model)
    depth = norm_ffn1.shape[0]
    for l in range(depth):
        last = l == depth - 1
        h = _ffn_call(h, norm_ffn1[l], ffn1_gate[l], ffn1_up[l], ffn1_down[l])
        h = _mixer_call(h, batch, seq, norm_mix[l], w_in[l], w_out[l], ret_gn_gain[l],
                        pool_w[l], pool_scale[l])
        h = _ffn_call(h, norm_ffn2[l], ffn2_gate[l], ffn2_up[l], ffn2_down[l],
                      final_gain=norm_final if last else None)
    return h.reshape(batch, seq, d_model)
```

```python
import functools

import numpy as np
import jax
import jax.numpy as jnp
from jax import lax
from jax.experimental import pallas as pl
from jax.experimental.pallas import tpu as pltpu

D_MODEL = 1024
D_FF = 2816
CHUNK = 64
RET_HEADS = 4
RET_DK = 64
RET_DV = 128
RET_QKWIDTH = RET_HEADS * RET_DK
RET_VWIDTH = RET_HEADS * RET_DV
POOL_WINDOWS = (2, 4, 8, 16)
POOL_GC = 128
POOL_WIDTH = len(POOL_WINDOWS) * POOL_GC
POOL_HALO = 16
IN_WIDTH = 2 * RET_QKWIDTH + 2 * RET_VWIDTH + POOL_WIDTH
ROPE_BASE = 10000.0
RMS_EPS = 1e-6
GN_EPS = 1e-5

RET_BLOCK = 256
FFN_TILE = 1024
FFN_SUB = 512
FF_CHUNK = 256
MIX_TILE = 1024
WEIGHT_CHUNKS = 16
WEIGHT_RING = 4

LANES = 128

_F32 = jnp.float32
_BF16 = jnp.bfloat16
_MIB = 1024 * 1024


def _rms_scale(x):
    return lax.rsqrt(jnp.mean(x * x, axis=-1, keepdims=True) + RMS_EPS)


def _rms_norm(x, gain):
    return x * _rms_scale(x) * gain


def _dot(a, b):
    return jnp.dot(a, b, preferred_element_type=_F32)


def _chunk_copy(pairs, stage, sem, j):
    ring, chunk_rows = stage.shape[0], stage.shape[1]
    chunks_per_weight = pairs[0][0].shape[0] // chunk_rows
    src, c = pairs[j // chunks_per_weight][0], j % chunks_per_weight
    return pltpu.make_async_copy(src.at[pl.ds(c * chunk_rows, chunk_rows), :],
                                 stage.at[j % ring], sem.at[j % ring])


def _load_weights_bf16(groups):
    for pairs, stage, sem in groups:
        for j in range(WEIGHT_RING - 1):
            _chunk_copy(pairs, stage, sem, j).start()
    for pairs, stage, sem in groups:
        chunk_rows = stage.shape[1]
        n_chunks = WEIGHT_CHUNKS * len(pairs)
        for j in range(n_chunks):
            if j + WEIGHT_RING - 1 < n_chunks:
                _chunk_copy(pairs, stage, sem, j + WEIGHT_RING - 1).start()
            _chunk_copy(pairs, stage, sem, j).wait()
            dst, c = pairs[j // WEIGHT_CHUNKS][1], j % WEIGHT_CHUNKS
            dst[c * chunk_rows:(c + 1) * chunk_rows, :] = stage[j % WEIGHT_RING].astype(_BF16)


def _weight_scratch(shapes):
    homes = [pltpu.VMEM(s, _BF16) for s in shapes]
    rings = []
    for s in dict.fromkeys(shapes):
        rings += [pltpu.VMEM((WEIGHT_RING, s[0] // WEIGHT_CHUNKS, s[1]), _F32),
                  pltpu.SemaphoreType.DMA((WEIGHT_RING,))]
    return homes + rings


def _weight_scratch_bytes(shapes):
    homes = sum(2 * s[0] * s[1] for s in shapes)
    rings = sum(WEIGHT_RING * 4 * (s[0] // WEIGHT_CHUNKS) * s[1] for s in dict.fromkeys(shapes))
    return homes + rings


def _ffn_kernel(x_ref, g_ref, wg_hbm, wu_hbm, wd_hbm, *rest, final_norm):
    if final_norm:
        gf_ref, *rest = rest
    o_ref, wg_ref, wu_ref, wd_ref, up_stage, up_sem, down_stage, down_sem = rest

    @pl.when(pl.program_id(0) == 0)
    def _():
        _load_weights_bf16([([(wg_hbm, wg_ref), (wu_hbm, wu_ref)], up_stage, up_sem),
                            ([(wd_hbm, wd_ref)], down_stage, down_sem)])

    for sub in range(x_ref.shape[0] // FFN_SUB):
        rows = slice(sub * FFN_SUB, (sub + 1) * FFN_SUB)
        x = x_ref[rows, :]
        xg = (x * g_ref[...]).astype(_BF16)
        r = jnp.broadcast_to(_rms_scale(x), (FFN_SUB, FF_CHUNK))
        hidden = []
        for c in range(D_FF // FF_CHUNK):
            cols = slice(c * FF_CHUNK, (c + 1) * FF_CHUNK)
            gate = _dot(xg, wg_ref[:, cols]) * r
            up = _dot(xg, wu_ref[:, cols]) * r
            hidden.append((gate * jax.nn.sigmoid(gate) * up).astype(_BF16))
        h = jnp.concatenate(hidden, axis=1)
        out = x + 0.5 * _dot(h, wd_ref[...])
        if final_norm:
            out = _rms_norm(out, gf_ref[...])
        o_ref[rows, :] = out


def _resident(shape):
    zeros = (0,) * len(shape)
    return pl.BlockSpec(shape, lambda *_: zeros, pipeline_mode=pl.Buffered(1))


_IN_HBM = pl.BlockSpec(memory_space=pl.ANY)


def _ffn_call(x2d, gain, wg, wu, wd, final_gain=None):
    rows = x2d.shape[0]
    tm = FFN_TILE
    final_norm = final_gain is not None
    row_spec = pl.BlockSpec((tm, D_MODEL), lambda i: (i, 0))
    in_specs = [row_spec, _resident((1, D_MODEL)), _IN_HBM, _IN_HBM, _IN_HBM]
    args = [x2d, gain.reshape(1, D_MODEL), wg, wu, wd]
    if final_norm:
        in_specs.append(_resident((1, D_MODEL)))
        args.append(final_gain.reshape(1, D_MODEL))
    weight_shapes = [wg.shape, wu.shape, wd.shape]
    io_bytes = 4 * tm * D_MODEL * 4
    temp_bytes = tm * D_FF * 2 * 2 + tm * D_MODEL * (2 + 4 + 4) + 8 * FFN_SUB * FF_CHUNK * 4
    return pl.pallas_call(
        functools.partial(_ffn_kernel, final_norm=final_norm),
        out_shape=jax.ShapeDtypeStruct((rows, D_MODEL), _F32),
        grid=(rows // tm,),
        in_specs=in_specs,
        out_specs=row_spec,
        scratch_shapes=_weight_scratch(weight_shapes),
        compiler_params=pltpu.CompilerParams(
            dimension_semantics=("arbitrary",),
            vmem_limit_bytes=(_weight_scratch_bytes(weight_shapes) + io_bytes + temp_bytes +
                              4 * _MIB)),
        name="ffn_final" if final_norm else "ffn",
    )(*args)


def _mixer_tables(seq):
    heads = np.arange(RET_HEADS)
    gamma = 1.0 - 2.0 ** (-5.0 - heads)
    i = np.arange(RET_BLOCK)
    diff = i[:, None] - i[None, :]
    same = (i[:, None] // CHUNK) == (i[None, :] // CHUNK)
    earlier = (i[None, :] // CHUNK) < (i[:, None] // CHUNK)
    expo = np.where(same, np.abs(diff), diff)
    q_scale = RET_DK ** -0.5
    dmat = np.where((same | earlier)[None], gamma[:, None, None] ** expo[None], 0.0) * q_scale
    dq = np.repeat((gamma[None, :] ** (i[:, None] + 1.0)) * q_scale, RET_DV, axis=1)
    dk = np.repeat(gamma[None, :] ** (RET_BLOCK - 1.0 - i[:, None]), RET_DK, axis=1)
    dc = np.repeat(gamma ** float(RET_BLOCK), RET_DV)[None, :]
    bm = (np.arange(RET_QKWIDTH)[:, None] // RET_DK ==
          np.arange(RET_VWIDTH)[None, :] // RET_DV).astype(np.float64)
    half = RET_DK // 2
    freqs = ROPE_BASE ** (-np.arange(half) * 2.0 / RET_DK)
    ang = np.arange(seq)[:, None] * freqs[None, :]
    cos = np.tile(np.cos(ang), (1, LANES // half))
    sin = np.tile(np.concatenate([-np.sin(ang), np.sin(ang)], axis=1), (1, LANES // RET_DK))
    f32 = lambda a: jnp.asarray(a, dtype=_F32)
    return tuple(f32(a) for a in (cos, sin, dmat, dq, dk, dc, bm))


def _rotary(t, cos, sin):
    lane = lax.broadcasted_iota(jnp.int32, (t.shape[0], LANES), 1)
    first_half = (lane & (RET_DK - 1)) < (RET_DK // 2)
    cols = []
    for c in range(t.shape[1] // LANES):
        tc = t[:, c * LANES:(c + 1) * LANES]
        partner = jnp.where(first_half, pltpu.roll(tc, LANES - RET_DK // 2, 1),
                            pltpu.roll(tc, RET_DK // 2, 1))
        cols.append(tc * cos + partner * sin)
    return jnp.concatenate(cols, axis=1)


def _mixer_kernel(x_ref, g_ref, win_hbm, wout_hbm, gain_ref, pw_ref, ps_ref,
                  cos_ref, sin_ref, dmat_ref, dq_ref, dk_ref, dc_ref, bm_ref,
                  o_ref, state_ref, u_ref, win_ref, wout_ref, in_stage, in_sem, out_stage, out_sem,
                  *, tm):
    s = pl.program_id(1)
    o1, o2, o3, o4 = RET_QKWIDTH, 2 * RET_QKWIDTH, 2 * RET_QKWIDTH + RET_VWIDTH, IN_WIDTH - POOL_WIDTH

    @pl.when((pl.program_id(0) == 0) & (s == 0))
    def _():
        _load_weights_bf16([([(win_hbm, win_ref)], in_stage, in_sem),
                            ([(wout_hbm, wout_ref)], out_stage, out_sem)])

    @pl.when(s == 0)
    def _():
        state_ref[...] = jnp.zeros_like(state_ref)
        u_ref[0:POOL_HALO, :] = jnp.zeros((POOL_HALO, POOL_WIDTH), _F32)

    @pl.when(s > 0)
    def _():
        u_ref[0:POOL_HALO, :] = u_ref[tm:tm + POOL_HALO, :]

    x = x_ref[...]
    n = (x * g_ref[...]).astype(_BF16)
    r_in = jnp.broadcast_to(_rms_scale(x), (tm, POOL_WIDTH))
    row0 = pl.multiple_of(s * tm, tm)

    u = _dot(n, win_ref[:, o4:]) * r_in
    u_ref[POOL_HALO:POOL_HALO + tm, :] = u
    pos = row0 + lax.broadcasted_iota(jnp.int32, (tm, 1), 0)
    pooled = []
    for gi, w in enumerate(POOL_WINDOWS):
        cs = slice(gi * POOL_GC, (gi + 1) * POOL_GC)
        acc = u_ref[:, cs]
        shift = 1
        while shift < w:
            acc = acc + pltpu.roll(acc, shift, 0)
            shift *= 2
        cnt = jnp.minimum(pos + 1, w).astype(_F32)
        pooled.append((acc[POOL_HALO:, :] / cnt - u[:, cs]).astype(_BF16))

    gate = _dot(n, win_ref[:, o3:o4]) * r_in
    gate = gate * jax.nn.sigmoid(gate)
    qkv = _dot(n, win_ref[:, :o3]) * jnp.concatenate([r_in, r_in], axis=1)
    pool = jnp.concatenate([_dot(pooled[gi], pw_ref[gi]) for gi in range(len(POOL_WINDOWS))],
                           axis=1)
    pool = (pool * ps_ref[...]).astype(_BF16)

    cos = cos_ref[pl.ds(row0, tm), :]
    sin = sin_ref[pl.ds(row0, tm), :]
    q = _rotary(qkv[:, :o1], cos, sin).astype(_BF16)
    k = _rotary(qkv[:, o1:o2], cos, sin)
    v = qkv[:, o2:o3].astype(_BF16)

    lane_head = lax.broadcasted_iota(jnp.int32, (RET_BLOCK, RET_QKWIDTH), 1) // RET_DK
    blocks = [slice(b * RET_BLOCK, (b + 1) * RET_BLOCK) for b in range(tm // RET_BLOCK)]
    states = [state_ref[...]]
    intra_blocks = []
    for rows in blocks:
        qb, kb, vb = q[rows], k[rows], v[rows]
        kb16 = kb.astype(_BF16)
        decayed = []
        for h in range(RET_HEADS):
            kh = jnp.where(lane_head == h, kb16, jnp.zeros_like(kb16))
            sc = lax.dot_general(qb, kh, (((1,), (1,)), ((), ())),
                                 preferred_element_type=_F32)
            decayed.append((sc * dmat_ref[h]).astype(_BF16))
        kd = (kb * dk_ref[...]).astype(_BF16)
        kv = lax.dot_general(kd, vb, (((0,), (0,)), ((), ())),
                             preferred_element_type=_F32)
        states.append(states[-1] * dc_ref[...] + kv * bm_ref[...])
        intra_blocks.append(jnp.concatenate(
            [_dot(decayed[h], vb[:, h * RET_DV:(h + 1) * RET_DV]) for h in range(RET_HEADS)],
            axis=1))
    state_ref[...] = states[-1]
    ret_blocks = []
    for rows, state, intra in zip(blocks, states, intra_blocks):
        cross = _dot(q[rows], state.astype(_BF16)) * dq_ref[...]
        ret_blocks.append(intra + cross)
    ret = ret_blocks[0] if len(ret_blocks) == 1 else jnp.concatenate(ret_blocks, axis=0)

    gn = []
    for h in range(RET_HEADS):
        o_h = ret[:, h * RET_DV:(h + 1) * RET_DV]
        mu = jnp.mean(o_h, axis=-1, keepdims=True)
        d = o_h - mu
        var = jnp.mean(d * d, axis=-1, keepdims=True)
        gn.append(d * lax.rsqrt(var + GN_EPS))
    r = gate * (jnp.concatenate(gn, axis=1) * gain_ref[...])

    mix = (_dot(r.astype(_BF16), wout_ref[0:RET_VWIDTH, :]) +
           _dot(pool, wout_ref[RET_VWIDTH:, :]))
    o_ref[...] = x + mix


def _mixer_call(x2d, batch, seq, gain, w_in, w_out, gn_gain, pool_w, pool_scale):
    tm = MIX_TILE
    steps = seq // tm
    row_spec = pl.BlockSpec((tm, D_MODEL), lambda b, s: (b * steps + s, 0))
    small = [gn_gain.reshape(1, RET_VWIDTH), pool_w.astype(_BF16),
             pool_scale.reshape(1, POOL_WIDTH), *_mixer_tables(seq)]
    in_specs = ([row_spec, _resident((1, D_MODEL)), _IN_HBM, _IN_HBM] +
                [_resident(a.shape) for a in small])
    weight_shapes = [w_in.shape, w_out.shape]
    resident_bytes = (sum(int(np.prod(a.shape)) * a.dtype.itemsize for a in small) +
                      _weight_scratch_bytes(weight_shapes))
    temp_bytes = tm * (IN_WIDTH * 4 * 2 + D_MODEL * 4 * 6) + 4 * RET_BLOCK * RET_BLOCK * 8
    return pl.pallas_call(
        functools.partial(_mixer_kernel, tm=tm),
        out_shape=jax.ShapeDtypeStruct(x2d.shape, _F32),
        grid=(batch, steps),
        in_specs=in_specs,
        out_specs=row_spec,
        scratch_shapes=[pltpu.VMEM((RET_QKWIDTH, RET_VWIDTH), _F32),
                        pltpu.VMEM((POOL_HALO + tm, POOL_WIDTH), _F32),
                        *_weight_scratch(weight_shapes)],
        compiler_params=pltpu.CompilerParams(
            dimension_semantics=("arbitrary", "arbitrary"),
            vmem_limit_bytes=resident_bytes + 4 * tm * D_MODEL * 4 + temp_bytes + 8 * _MIB),
        name="mixer",
    )(x2d, gain.reshape(1, D_MODEL), w_in, w_out, *small)


def kernel(x, norm_ffn1, ffn1_gate, ffn1_up, ffn1_down, norm_mix, w_in, ret_gn_gain, pool_w,
           pool_scale, w_out, norm_ffn2, ffn2_gate, ffn2_up, ffn2_down, norm_final):
    batch, seq, d_model = x.shape
    assert d_model == D_MODEL and seq % MIX_TILE == 0 and (batch * seq) % FFN_TILE == 0
    h = x.reshape(batch * seq, d_model)
    depth = norm_ffn1.shape[0]
    for l in range(depth):
        last = l == depth - 1
        h = _ffn_call(h, norm_ffn1[l], ffn1_gate[l], ffn1_up[l], ffn1_down[l])
        h = _mixer_call(h, batch, seq, norm_mix[l], w_in[l], w_out[l], ret_gn_gain[l],
                        pool_w[l], pool_scale[l])
        h = _ffn_call(h, norm_ffn2[l], ffn2_gate[l], ffn2_up[l], ffn2_down[l],
                      final_gain=norm_final if last else None)
    return h.reshape(batch, seq, d_model)
```

```python
import functools

import numpy as np
import jax
import jax.numpy as jnp
from jax import lax
from jax.experimental import pallas as pl
from jax.experimental.pallas import tpu as pltpu

D_MODEL = 1024
D_FF = 2816
CHUNK = 64
RET_HEADS = 4
RET_DK = 64
RET_DV = 128
RET_QKWIDTH = RET_HEADS * RET_DK
RET_VWIDTH = RET_HEADS * RET_DV
POOL_WINDOWS = (2, 4, 8, 16)
POOL_GC = 128
POOL_WIDTH = len(POOL_WINDOWS) * POOL_GC
POOL_HALO = 16
IN_WIDTH = 2 * RET_QKWIDTH + 2 * RET_VWIDTH + POOL_WIDTH
ROPE_BASE = 10000.0
RMS_EPS = 1e-6
GN_EPS = 1e-5

RET_BLOCK = 256
FFN_TILE = 1024
FFN_SUB = 512
FF_CHUNK = 256
MIX_TILE = 1024
WEIGHT_CHUNKS = 16
WEIGHT_RING = 4

LANES = 128

_F32 = jnp.float32
_BF16 = jnp.bfloat16
_MIB = 1024 * 1024


def _rms_scale(x):
    return lax.rsqrt(jnp.mean(x * x, axis=-1, keepdims=True) + RMS_EPS)


def _rms_norm(x, gain):
    return x * _rms_scale(x) * gain


def _dot(a, b):
    return jnp.dot(a, b, preferred_element_type=_F32)


def _chunk_copy(pairs, stage, sem, j):
    ring, chunk_rows = stage.shape[0], stage.shape[1]
    chunks_per_weight = pairs[0][0].shape[0] // chunk_rows
    src, c = pairs[j // chunks_per_weight][0], j % chunks_per_weight
    return pltpu.make_async_copy(src.at[pl.ds(c * chunk_rows, chunk_rows), :],
                                 stage.at[j % ring], sem.at[j % ring])


def _load_weights_bf16(groups):
    for pairs, stage, sem in groups:
        for j in range(WEIGHT_RING - 1):
            _chunk_copy(pairs, stage, sem, j).start()
    for pairs, stage, sem in groups:
        chunk_rows = stage.shape[1]
        n_chunks = WEIGHT_CHUNKS * len(pairs)
        for j in range(n_chunks):
            if j + WEIGHT_RING - 1 < n_chunks:
                _chunk_copy(pairs, stage, sem, j + WEIGHT_RING - 1).start()
            _chunk_copy(pairs, stage, sem, j).wait()
            dst, c = pairs[j // WEIGHT_CHUNKS][1], j % WEIGHT_CHUNKS
            dst[c * chunk_rows:(c + 1) * chunk_rows, :] = stage[j % WEIGHT_RING].astype(_BF16)


def _weight_scratch(shapes):
    homes = [pltpu.VMEM(s, _BF16) for s in shapes]
    rings = []
    for s in dict.fromkeys(shapes):
        rings += [pltpu.VMEM((WEIGHT_RING, s[0] // WEIGHT_CHUNKS, s[1]), _F32),
                  pltpu.SemaphoreType.DMA((WEIGHT_RING,))]
    return homes + rings


def _weight_scratch_bytes(shapes):
    homes = sum(2 * s[0] * s[1] for s in shapes)
    rings = sum(WEIGHT_RING * 4 * (s[0] // WEIGHT_CHUNKS) * s[1] for s in dict.fromkeys(shapes))
    return homes + rings


def _ffn_kernel(x_ref, g_ref, wg_hbm, wu_hbm, wd_hbm, *rest, final_norm):
    if final_norm:
        gf_ref, *rest = rest
    o_ref, wg_ref, wu_ref, wd_ref, up_stage, up_sem, down_stage, down_sem = rest

    @pl.when(pl.program_id(0) == 0)
    def _():
        _load_weights_bf16([([(wg_hbm, wg_ref), (wu_hbm, wu_ref)], up_stage, up_sem),
                            ([(wd_hbm, wd_ref)], down_stage, down_sem)])

    for sub in range(x_ref.shape[0] // FFN_SUB):
        rows = slice(sub * FFN_SUB, (sub + 1) * FFN_SUB)
        x = x_ref[rows, :]
        xg = (x * g_ref[...]).astype(_BF16)
        r = jnp.broadcast_to(_rms_scale(x), (FFN_SUB, FF_CHUNK))
        hidden = []
        for c in range(D_FF // FF_CHUNK):
            cols = slice(c * FF_CHUNK, (c + 1) * FF_CHUNK)
            gate = _dot(xg, wg_ref[:, cols]) * r
            up = _dot(xg, wu_ref[:, cols]) * r
            hidden.append((gate * jax.nn.sigmoid(gate) * up).astype(_BF16))
        h = jnp.concatenate(hidden, axis=1)
        if final_norm:
            half = FFN_SUB // 2
            for lo in (0, half):
                out = x[lo:lo + half] + 0.5 * _dot(h[lo:lo + half], wd_ref[...])
                o_ref[rows.start + lo:rows.start + lo + half, :] = _rms_norm(out, gf_ref[...])
        else:
            o_ref[rows, :] = x + 0.5 * _dot(h, wd_ref[...])


def _resident(shape):
    zeros = (0,) * len(shape)
    return pl.BlockSpec(shape, lambda *_: zeros, pipeline_mode=pl.Buffered(1))


_IN_HBM = pl.BlockSpec(memory_space=pl.ANY)


def _ffn_call(x2d, gain, wg, wu, wd, final_gain=None):
    rows = x2d.shape[0]
    tm = FFN_TILE
    final_norm = final_gain is not None
    row_spec = pl.BlockSpec((tm, D_MODEL), lambda i: (i, 0))
    in_specs = [row_spec, _resident((1, D_MODEL)), _IN_HBM, _IN_HBM, _IN_HBM]
    args = [x2d, gain.reshape(1, D_MODEL), wg, wu, wd]
    if final_norm:
        in_specs.append(_resident((1, D_MODEL)))
        args.append(final_gain.reshape(1, D_MODEL))
    weight_shapes = [wg.shape, wu.shape, wd.shape]
    io_bytes = 4 * tm * D_MODEL * 4
    temp_bytes = tm * D_FF * 2 * 2 + tm * D_MODEL * (2 + 4 + 4) + 8 * FFN_SUB * FF_CHUNK * 4
    return pl.pallas_call(
        functools.partial(_ffn_kernel, final_norm=final_norm),
        out_shape=jax.ShapeDtypeStruct((rows, D_MODEL), _F32),
        grid=(rows // tm,),
        in_specs=in_specs,
        out_specs=row_spec,
        scratch_shapes=_weight_scratch(weight_shapes),
        compiler_params=pltpu.CompilerParams(
            dimension_semantics=("arbitrary",),
            vmem_limit_bytes=(_weight_scratch_bytes(weight_shapes) + io_bytes + temp_bytes +
                              4 * _MIB)),
        name="ffn_final" if final_norm else "ffn",
    )(*args)


def _mixer_tables(seq):
    heads = np.arange(RET_HEADS)
    gamma = 1.0 - 2.0 ** (-5.0 - heads)
    i = np.arange(RET_BLOCK)
    diff = i[:, None] - i[None, :]
    same = (i[:, None] // CHUNK) == (i[None, :] // CHUNK)
    earlier = (i[None, :] // CHUNK) < (i[:, None] // CHUNK)
    expo = np.where(same, np.abs(diff), diff)
    q_scale = RET_DK ** -0.5
    dmat = np.where((same | earlier)[None], gamma[:, None, None] ** expo[None], 0.0) * q_scale
    dq = np.repeat((gamma[None, :] ** (i[:, None] + 1.0)) * q_scale, RET_DV, axis=1)
    dk = np.repeat(gamma[None, :] ** (RET_BLOCK - 1.0 - i[:, None]), RET_DK, axis=1)
    dc = np.repeat(gamma ** float(RET_BLOCK), RET_DK)[:, None] * np.ones((1, RET_DV))
    half = RET_DK // 2
    freqs = ROPE_BASE ** (-np.arange(half) * 2.0 / RET_DK)
    ang = np.arange(seq)[:, None] * freqs[None, :]
    cos = np.tile(np.cos(ang), (1, LANES // half))
    sin = np.tile(np.concatenate([-np.sin(ang), np.sin(ang)], axis=1), (1, LANES // RET_DK))
    f32 = lambda a: jnp.asarray(a, dtype=_F32)
    return tuple(f32(a) for a in (cos, sin, dmat, dq, dk, dc))


def _rotary(t, cos, sin):
    lane = lax.broadcasted_iota(jnp.int32, (t.shape[0], LANES), 1)
    first_half = (lane & (RET_DK - 1)) < (RET_DK // 2)
    cols = []
    for c in range(t.shape[1] // LANES):
        tc = t[:, c * LANES:(c + 1) * LANES]
        partner = jnp.where(first_half, pltpu.roll(tc, LANES - RET_DK // 2, 1),
                            pltpu.roll(tc, RET_DK // 2, 1))
        cols.append(tc * cos + partner * sin)
    return jnp.concatenate(cols, axis=1)


def _mixer_kernel(x_ref, g_ref, win_hbm, wout_hbm, gain_ref, pw_ref, ps_ref,
                  cos_ref, sin_ref, dmat_ref, dq_ref, dk_ref, dc_ref,
                  o_ref, state_ref, u_ref, win_ref, wout_ref, in_stage, in_sem, out_stage, out_sem,
                  *, tm):
    s = pl.program_id(1)
    o1, o2, o3, o4 = RET_QKWIDTH, 2 * RET_QKWIDTH, 2 * RET_QKWIDTH + RET_VWIDTH, IN_WIDTH - POOL_WIDTH

    @pl.when((pl.program_id(0) == 0) & (s == 0))
    def _():
        _load_weights_bf16([([(win_hbm, win_ref)], in_stage, in_sem),
                            ([(wout_hbm, wout_ref)], out_stage, out_sem)])

    @pl.when(s == 0)
    def _():
        state_ref[...] = jnp.zeros_like(state_ref)
        u_ref[0:POOL_HALO, :] = jnp.zeros((POOL_HALO, POOL_WIDTH), _F32)

    @pl.when(s > 0)
    def _():
        u_ref[0:POOL_HALO, :] = u_ref[tm:tm + POOL_HALO, :]

    x = x_ref[...]
    n = (x * g_ref[...]).astype(_BF16)
    r_in = jnp.broadcast_to(_rms_scale(x), (tm, POOL_WIDTH))
    row0 = pl.multiple_of(s * tm, tm)

    u = _dot(n, win_ref[:, o4:]) * r_in
    u_ref[POOL_HALO:POOL_HALO + tm, :] = u
    pos = row0 + lax.broadcasted_iota(jnp.int32, (tm, 1), 0)
    pooled = []
    for gi, w in enumerate(POOL_WINDOWS):
        cs = slice(gi * POOL_GC, (gi + 1) * POOL_GC)
        acc = u_ref[:, cs]
        shift = 1
        while shift < w:
            acc = acc + pltpu.roll(acc, shift, 0)
            shift *= 2
        cnt = jnp.minimum(pos + 1, w).astype(_F32)
        pooled.append((acc[POOL_HALO:, :] / cnt - u[:, cs]).astype(_BF16))

    gate = _dot(n, win_ref[:, o3:o4]) * r_in
    gate = gate * jax.nn.sigmoid(gate)
    qkv = _dot(n, win_ref[:, :o3]) * jnp.concatenate([r_in, r_in], axis=1)
    pool = jnp.concatenate([_dot(pooled[gi], pw_ref[gi]) for gi in range(len(POOL_WINDOWS))],
                           axis=1)
    pool = (pool * ps_ref[...]).astype(_BF16)

    cos = cos_ref[pl.ds(row0, tm), :]
    sin = sin_ref[pl.ds(row0, tm), :]
    q = _rotary(qkv[:, :o1], cos, sin).astype(_BF16)
    k = _rotary(qkv[:, o1:o2], cos, sin)
    v = qkv[:, o2:o3].astype(_BF16)

    lane_head = lax.broadcasted_iota(jnp.int32, (RET_BLOCK, RET_QKWIDTH), 1) // RET_DK
    blocks = [slice(b * RET_BLOCK, (b + 1) * RET_BLOCK) for b in range(tm // RET_BLOCK)]
    states = [state_ref[...]]
    intra_blocks = []
    for rows in blocks:
        qb, kb, vb = q[rows], k[rows], v[rows]
        kb16 = kb.astype(_BF16)
        decayed = []
        for h in range(RET_HEADS):
            kh = jnp.where(lane_head == h, kb16, jnp.zeros_like(kb16))
            sc = lax.dot_general(qb, kh, (((1,), (1,)), ((), ())),
                                 preferred_element_type=_F32)
            decayed.append((sc * dmat_ref[h]).astype(_BF16))
        kd = (kb * dk_ref[...]).astype(_BF16)
        kv = lax.dot_general(kd, vb, (((0,), (0,)), ((), ())),
                             preferred_element_type=_F32)
        kv_diag = jnp.concatenate([kv[h * RET_DK:(h + 1) * RET_DK, h * RET_DV:(h + 1) * RET_DV]
                                   for h in range(RET_HEADS)], axis=0)
        states.append(states[-1] * dc_ref[...] + kv_diag)
        intra_blocks.append(jnp.concatenate(
            [_dot(decayed[h], vb[:, h * RET_DV:(h + 1) * RET_DV]) for h in range(RET_HEADS)],
            axis=1))
    state_ref[...] = states[-1]
    ret_blocks = []
    zero_block = jnp.zeros((RET_DK, RET_DV), _BF16)
    for rows, state, intra in zip(blocks, states, intra_blocks):
        state16 = state.astype(_BF16)
        expanded = jnp.concatenate(
            [jnp.concatenate([state16[h * RET_DK:(h + 1) * RET_DK] if j == h else zero_block
                              for j in range(RET_HEADS)], axis=1) for h in range(RET_HEADS)], axis=0)
        cross = _dot(q[rows], expanded) * dq_ref[...]
        ret_blocks.append(intra + cross)
    ret = ret_blocks[0] if len(ret_blocks) == 1 else jnp.concatenate(ret_blocks, axis=0)

    gn = []
    for h in range(RET_HEADS):
        o_h = ret[:, h * RET_DV:(h + 1) * RET_DV]
        mu = jnp.mean(o_h, axis=-1, keepdims=True)
        d = o_h - mu
        var = jnp.mean(d * d, axis=-1, keepdims=True)
        gn.append(d * lax.rsqrt(var + GN_EPS))
    r = gate * (jnp.concatenate(gn, axis=1) * gain_ref[...])

    mix = (_dot(r.astype(_BF16), wout_ref[0:RET_VWIDTH, :]) +
           _dot(pool, wout_ref[RET_VWIDTH:, :]))
    o_ref[...] = x + mix


def _mixer_call(x2d, batch, seq, gain, w_in, w_out, gn_gain, pool_w, pool_scale):
    tm = MIX_TILE
    steps = seq // tm
    row_spec = pl.BlockSpec((tm, D_MODEL), lambda b, s: (b * steps + s, 0))
    small = [gn_gain.reshape(1, RET_VWIDTH), pool_w.astype(_BF16),
             pool_scale.reshape(1, POOL_WIDTH), *_mixer_tables(seq)]
    in_specs = ([row_spec, _resident((1, D_MODEL)), _IN_HBM, _IN_HBM] +
                [_resident(a.shape) for a in small])
    weight_shapes = [w_in.shape, w_out.shape]
    resident_bytes = (sum(int(np.prod(a.shape)) * a.dtype.itemsize for a in small) +
                      _weight_scratch_bytes(weight_shapes))
    temp_bytes = tm * (IN_WIDTH * 4 * 2 + D_MODEL * 4 * 6) + 4 * RET_BLOCK * RET_BLOCK * 8
    return pl.pallas_call(
        functools.partial(_mixer_kernel, tm=tm),
        out_shape=jax.ShapeDtypeStruct(x2d.shape, _F32),
        grid=(batch, steps),
        in_specs=in_specs,
        out_specs=row_spec,
        scratch_shapes=[pltpu.VMEM((RET_QKWIDTH, RET_DV), _F32),
                        pltpu.VMEM((POOL_HALO + tm, POOL_WIDTH), _F32),
                        *_weight_scratch(weight_shapes)],
        compiler_params=pltpu.CompilerParams(
            dimension_semantics=("arbitrary", "arbitrary"),
            vmem_limit_bytes=resident_bytes + 4 * tm * D_MODEL * 4 + temp_bytes + 8 * _MIB),
        name="mixer",
    )(x2d, gain.reshape(1, D_MODEL), w_in, w_out, *small)


def kernel(x, norm_ffn1, ffn1_gate, ffn1_up, ffn1_down, norm_mix, w_in, ret_gn_gain, pool_w,
           pool_scale, w_out, norm_ffn2, ffn2_gate, ffn2_up, ffn2_down, norm_final):
    batch, seq, d_model = x.shape
    assert d_model == D_MODEL and seq % MIX_TILE == 0 and (batch * seq) % FFN_TILE == 0
    h = x.reshape(batch * seq, d_model)
    depth = norm_ffn1.shape[0]
    for l in range(depth):
        last = l == depth - 1
        h = _ffn_call(h, norm_ffn1[l], ffn1_gate[l], ffn1_up[l], ffn1_down[l])
        h = _mixer_call(h, batch, seq, norm_mix[l], w_in[l], w_out[l], ret_gn_gain[l],
                        pool_w[l], pool_scale[l])
        h = _ffn_call(h, norm_ffn2[l], ffn2_gate[l], ffn2_up[l], ffn2_down[l],
                      final_gain=norm_final if last else None)
    return h.reshape(batch, seq, d_model)
```

```python
import functools

import numpy as np
import jax
import jax.numpy as jnp
from jax import lax
from jax.experimental import pallas as pl
from jax.experimental.pallas import tpu as pltpu

D_MODEL = 1024
D_FF = 2816
CHUNK = 64
RET_HEADS = 4
RET_DK = 64
RET_DV = 128
RET_QKWIDTH = RET_HEADS * RET_DK
RET_VWIDTH = RET_HEADS * RET_DV
POOL_WINDOWS = (2, 4, 8, 16)
POOL_GC = 128
POOL_WIDTH = len(POOL_WINDOWS) * POOL_GC
POOL_HALO = 16
IN_WIDTH = 2 * RET_QKWIDTH + 2 * RET_VWIDTH + POOL_WIDTH
ROPE_BASE = 10000.0
RMS_EPS = 1e-6
GN_EPS = 1e-5

RET_BLOCK = 256
FFN_TILE = 1024
FFN_SUB = 512
FF_CHUNK = 256
MIX_TILE = 1024
WEIGHT_CHUNKS = 16
WEIGHT_RING = 4

LANES = 128

_F32 = jnp.float32
_BF16 = jnp.bfloat16
_MIB = 1024 * 1024


def _rms_scale(x):
    return lax.rsqrt(jnp.mean(x * x, axis=-1, keepdims=True) + RMS_EPS)


def _rms_norm(x, gain):
    return x * _rms_scale(x) * gain


def _dot(a, b):
    return jnp.dot(a, b, preferred_element_type=_F32)


def _chunk_copy(pairs, stage, sem, j):
    ring, chunk_rows = stage.shape[0], stage.shape[1]
    chunks_per_weight = pairs[0][0].shape[0] // chunk_rows
    src, c = pairs[j // chunks_per_weight][0], j % chunks_per_weight
    return pltpu.make_async_copy(src.at[pl.ds(c * chunk_rows, chunk_rows), :],
                                 stage.at[j % ring], sem.at[j % ring])


def _load_weights_bf16(groups):
    for pairs, stage, sem in groups:
        for j in range(WEIGHT_RING - 1):
            _chunk_copy(pairs, stage, sem, j).start()
    for pairs, stage, sem in groups:
        chunk_rows = stage.shape[1]
        n_chunks = WEIGHT_CHUNKS * len(pairs)
        for j in range(n_chunks):
            if j + WEIGHT_RING - 1 < n_chunks:
                _chunk_copy(pairs, stage, sem, j + WEIGHT_RING - 1).start()
            _chunk_copy(pairs, stage, sem, j).wait()
            dst, c = pairs[j // WEIGHT_CHUNKS][1], j % WEIGHT_CHUNKS
            dst[c * chunk_rows:(c + 1) * chunk_rows, :] = stage[j % WEIGHT_RING].astype(_BF16)


def _weight_scratch(shapes):
    homes = [pltpu.VMEM(s, _BF16) for s in shapes]
    rings = []
    for s in dict.fromkeys(shapes):
        rings += [pltpu.VMEM((WEIGHT_RING, s[0] // WEIGHT_CHUNKS, s[1]), _F32),
                  pltpu.SemaphoreType.DMA((WEIGHT_RING,))]
    return homes + rings


def _weight_scratch_bytes(shapes):
    homes = sum(2 * s[0] * s[1] for s in shapes)
    rings = sum(WEIGHT_RING * 4 * (s[0] // WEIGHT_CHUNKS) * s[1] for s in dict.fromkeys(shapes))
    return homes + rings


def _ffn_kernel(x_ref, g_ref, wg_hbm, wu_hbm, wd_hbm, *rest, final_norm):
    if final_norm:
        gf_ref, *rest = rest
    o_ref, wg_ref, wu_ref, wd_ref, up_stage, up_sem, down_stage, down_sem = rest

    @pl.when(pl.program_id(0) == 0)
    def _():
        _load_weights_bf16([([(wg_hbm, wg_ref), (wu_hbm, wu_ref)], up_stage, up_sem),
                            ([(wd_hbm, wd_ref)], down_stage, down_sem)])

    for sub in range(x_ref.shape[0] // FFN_SUB):
        rows = slice(sub * FFN_SUB, (sub + 1) * FFN_SUB)
        x = x_ref[rows, :]
        xg = (x * g_ref[...]).astype(_BF16)
        r = jnp.broadcast_to(_rms_scale(x), (FFN_SUB, FF_CHUNK))
        hidden = []
        for c in range(D_FF // FF_CHUNK):
            cols = slice(c * FF_CHUNK, (c + 1) * FF_CHUNK)
            gate = _dot(xg, wg_ref[:, cols]) * r
            up = _dot(xg, wu_ref[:, cols]) * r
            hidden.append((gate * jax.nn.sigmoid(gate) * up).astype(_BF16))
        h = jnp.concatenate(hidden, axis=1)
        out = x + 0.5 * _dot(h, wd_ref[...])
        if final_norm:
            out = _rms_norm(out, gf_ref[...])
        o_ref[rows, :] = out


def _resident(shape):
    zeros = (0,) * len(shape)
    return pl.BlockSpec(shape, lambda *_: zeros, pipeline_mode=pl.Buffered(1))


_IN_HBM = pl.BlockSpec(memory_space=pl.ANY)


def _ffn_call(x2d, gain, wg, wu, wd, final_gain=None):
    rows = x2d.shape[0]
    tm = FFN_TILE
    final_norm = final_gain is not None
    row_spec = pl.BlockSpec((tm, D_MODEL), lambda i: (i, 0))
    in_specs = [row_spec, _resident((1, D_MODEL)), _IN_HBM, _IN_HBM, _IN_HBM]
    args = [x2d, gain.reshape(1, D_MODEL), wg, wu, wd]
    if final_norm:
        in_specs.append(_resident((1, D_MODEL)))
        args.append(final_gain.reshape(1, D_MODEL))
    weight_shapes = [wg.shape, wu.shape, wd.shape]
    io_bytes = 4 * tm * D_MODEL * 4
    temp_bytes = tm * D_FF * 2 * 2 + tm * D_MODEL * (2 + 4 + 4) + 8 * FFN_SUB * FF_CHUNK * 4
    return pl.pallas_call(
        functools.partial(_ffn_kernel, final_norm=final_norm),
        out_shape=jax.ShapeDtypeStruct((rows, D_MODEL), _F32),
        grid=(rows // tm,),
        in_specs=in_specs,
        out_specs=row_spec,
        scratch_shapes=_weight_scratch(weight_shapes),
        compiler_params=pltpu.CompilerParams(
            dimension_semantics=("arbitrary",),
            vmem_limit_bytes=(_weight_scratch_bytes(weight_shapes) + io_bytes + temp_bytes +
                              4 * _MIB)),
        name="ffn_final" if final_norm else "ffn",
    )(*args)


def _mixer_tables(seq):
    heads = np.arange(RET_HEADS)
    gamma = 1.0 - 2.0 ** (-5.0 - heads)
    i = np.arange(RET_BLOCK)
    diff = i[:, None] - i[None, :]
    same = (i[:, None] // CHUNK) == (i[None, :] // CHUNK)
    earlier = (i[None, :] // CHUNK) < (i[:, None] // CHUNK)
    expo = np.where(same, np.abs(diff), diff)
    q_scale = RET_DK ** -0.5
    dmat = np.where((same | earlier)[None], gamma[:, None, None] ** expo[None], 0.0) * q_scale
    dq = np.repeat((gamma[None, :] ** (i[:, None] + 1.0)) * q_scale, RET_DV, axis=1)
    dk = np.repeat(gamma[None, :] ** (RET_BLOCK - 1.0 - i[:, None]), RET_DK, axis=1)
    dc = np.repeat(gamma ** float(RET_BLOCK), RET_DK)[:, None] * np.ones((1, RET_DV))
    half = RET_DK // 2
    freqs = ROPE_BASE ** (-np.arange(half) * 2.0 / RET_DK)
    ang = np.arange(seq)[:, None] * freqs[None, :]
    cos = np.tile(np.cos(ang), (1, LANES // half))
    sin = np.tile(np.concatenate([-np.sin(ang), np.sin(ang)], axis=1), (1, LANES // RET_DK))
    f32 = lambda a: jnp.asarray(a, dtype=_F32)
    return tuple(f32(a) for a in (cos, sin, dmat, dq, dk, dc))


def _rotary(t, cos, sin):
    lane = lax.broadcasted_iota(jnp.int32, (t.shape[0], LANES), 1)
    first_half = (lane & (RET_DK - 1)) < (RET_DK // 2)
    cols = []
    for c in range(t.shape[1] // LANES):
        tc = t[:, c * LANES:(c + 1) * LANES]
        partner = jnp.where(first_half, pltpu.roll(tc, LANES - RET_DK // 2, 1),
                            pltpu.roll(tc, RET_DK // 2, 1))
        cols.append(tc * cos + partner * sin)
    return jnp.concatenate(cols, axis=1)


def _mixer_kernel(x_ref, g_ref, win_hbm, wout_hbm, gain_ref, pw_ref, ps_ref,
                  cos_ref, sin_ref, dmat_ref, dq_ref, dk_ref, dc_ref,
                  o_ref, state_ref, u_ref, win_ref, wout_ref, in_stage, in_sem, out_stage, out_sem,
                  *, tm):
    s = pl.program_id(1)
    o1, o2, o3, o4 = RET_QKWIDTH, 2 * RET_QKWIDTH, 2 * RET_QKWIDTH + RET_VWIDTH, IN_WIDTH - POOL_WIDTH

    @pl.when((pl.program_id(0) == 0) & (s == 0))
    def _():
        _load_weights_bf16([([(win_hbm, win_ref)], in_stage, in_sem),
                            ([(wout_hbm, wout_ref)], out_stage, out_sem)])

    @pl.when(s == 0)
    def _():
        state_ref[...] = jnp.zeros_like(state_ref)
        u_ref[0:POOL_HALO, :] = jnp.zeros((POOL_HALO, POOL_WIDTH), _F32)

    @pl.when(s > 0)
    def _():
        u_ref[0:POOL_HALO, :] = u_ref[tm:tm + POOL_HALO, :]

    x = x_ref[...]
    n = (x * g_ref[...]).astype(_BF16)
    r_in = jnp.broadcast_to(_rms_scale(x), (tm, POOL_WIDTH))
    row0 = pl.multiple_of(s * tm, tm)

    u = _dot(n, win_ref[:, o4:]) * r_in
    u_ref[POOL_HALO:POOL_HALO + tm, :] = u
    pos = row0 + lax.broadcasted_iota(jnp.int32, (tm, 1), 0)
    pooled = []
    for gi, w in enumerate(POOL_WINDOWS):
        cs = slice(gi * POOL_GC, (gi + 1) * POOL_GC)
        acc = u_ref[:, cs]
        shift = 1
        while shift < w:
            acc = acc + pltpu.roll(acc, shift, 0)
            shift *= 2
        cnt = jnp.minimum(pos + 1, w).astype(_F32)
        pooled.append((acc[POOL_HALO:, :] / cnt - u[:, cs]).astype(_BF16))

    gate = _dot(n, win_ref[:, o3:o4]) * r_in
    gate = gate * jax.nn.sigmoid(gate)
    qkv = _dot(n, win_ref[:, :o3]) * jnp.concatenate([r_in, r_in], axis=1)
    pool = jnp.concatenate([_dot(pooled[gi], pw_ref[gi]) for gi in range(len(POOL_WINDOWS))],
                           axis=1)
    pool = (pool * ps_ref[...]).astype(_BF16)

    cos = cos_ref[pl.ds(row0, tm), :]
    sin = sin_ref[pl.ds(row0, tm), :]
    q = _rotary(qkv[:, :o1], cos, sin).astype(_BF16)
    k = _rotary(qkv[:, o1:o2], cos, sin)
    v = qkv[:, o2:o3].astype(_BF16)

    lane_head = lax.broadcasted_iota(jnp.int32, (RET_BLOCK, RET_QKWIDTH), 1) // RET_DK
    blocks = [slice(b * RET_BLOCK, (b + 1) * RET_BLOCK) for b in range(tm // RET_BLOCK)]
    states = [state_ref[...]]
    intra_blocks = []
    for rows in blocks:
        qb, kb, vb = q[rows], k[rows], v[rows]
        kb16 = kb.astype(_BF16)
        decayed = []
        for h in range(RET_HEADS):
            kh = jnp.where(lane_head == h, kb16, jnp.zeros_like(kb16))
            sc = lax.dot_general(qb, kh, (((1,), (1,)), ((), ())),
                                 preferred_element_type=_F32)
            decayed.append((sc * dmat_ref[h]).astype(_BF16))
        kd = (kb * dk_ref[...]).astype(_BF16)
        kv = lax.dot_general(kd, vb, (((0,), (0,)), ((), ())),
                             preferred_element_type=_F32)
        kv_diag = jnp.concatenate([kv[h * RET_DK:(h + 1) * RET_DK, h * RET_DV:(h + 1) * RET_DV]
                                   for h in range(RET_HEADS)], axis=0)
        states.append(states[-1] * dc_ref[...] + kv_diag)
        intra_blocks.append(jnp.concatenate(
            [_dot(decayed[h], vb[:, h * RET_DV:(h + 1) * RET_DV]) for h in range(RET_HEADS)],
            axis=1))
    state_ref[...] = states[-1]
    ret_blocks = []
    zero_block = jnp.zeros((RET_DK, RET_DV), _BF16)
    for rows, state, intra in zip(blocks, states, intra_blocks):
        state16 = state.astype(_BF16)
        expanded = jnp.concatenate(
            [jnp.concatenate([state16[h * RET_DK:(h + 1) * RET_DK] if j == h else zero_block
                              for j in range(RET_HEADS)], axis=1) for h in range(RET_HEADS)], axis=0)
        cross = _dot(q[rows], expanded) * dq_ref[...]
        ret_blocks.append(intra + cross)
    ret = ret_blocks[0] if len(ret_blocks) == 1 else jnp.concatenate(ret_blocks, axis=0)

    gn = []
    for h in range(RET_HEADS):
        o_h = ret[:, h * RET_DV:(h + 1) * RET_DV]
        mu = jnp.mean(o_h, axis=-1, keepdims=True)
        d = o_h - mu
        var = jnp.mean(d * d, axis=-1, keepdims=True)
        gn.append(d * lax.rsqrt(var + GN_EPS))
    r = gate * (jnp.concatenate(gn, axis=1) * gain_ref[...])

    mix = (_dot(r.astype(_BF16), wout_ref[0:RET_VWIDTH, :]) +
           _dot(pool, wout_ref[RET_VWIDTH:, :]))
    o_ref[...] = x + mix


def _mixer_call(x2d, batch, seq, gain, w_in, w_out, gn_gain, pool_w, pool_scale):
    tm = MIX_TILE
    steps = seq // tm
    row_spec = pl.BlockSpec((tm, D_MODEL), lambda b, s: (b * steps + s, 0))
    small = [gn_gain.reshape(1, RET_VWIDTH), pool_w.astype(_BF16),
             pool_scale.reshape(1, POOL_WIDTH), *_mixer_tables(seq)]
    in_specs = ([row_spec, _resident((1, D_MODEL)), _IN_HBM, _IN_HBM] +
                [_resident(a.shape) for a in small])
    weight_shapes = [w_in.shape, w_out.shape]
    resident_bytes = (sum(int(np.prod(a.shape)) * a.dtype.itemsize for a in small) +
                      _weight_scratch_bytes(weight_shapes))
    temp_bytes = tm * (IN_WIDTH * 4 * 2 + D_MODEL * 4 * 6) + 4 * RET_BLOCK * RET_BLOCK * 8
    return pl.pallas_call(
        functools.partial(_mixer_kernel, tm=tm),
        out_shape=jax.ShapeDtypeStruct(x2d.shape, _F32),
        grid=(batch, steps),
        in_specs=in_specs,
        out_specs=row_spec,
        scratch_shapes=[pltpu.VMEM((RET_QKWIDTH, RET_DV), _F32),
                        pltpu.VMEM((POOL_HALO + tm, POOL_WIDTH), _F32),
                        *_weight_scratch(weight_shapes)],
        compiler_params=pltpu.CompilerParams(
            dimension_semantics=("arbitrary", "arbitrary"),
            vmem_limit_bytes=resident_bytes + 4 * tm * D_MODEL * 4 + temp_bytes + 8 * _MIB),
        name="mixer",
    )(x2d, gain.reshape(1, D_MODEL), w_in, w_out, *small)


def kernel(x, norm_ffn1, ffn1_gate, ffn1_up, ffn1_down, norm_mix, w_in, ret_gn_gain, pool_w,
           pool_scale, w_out, norm_ffn2, ffn2_gate, ffn2_up, ffn2_down, norm_final):
    batch, seq, d_model = x.shape
    assert d_model == D_MODEL and seq % MIX_TILE == 0 and (batch * seq) % FFN_TILE == 0
    h = x.reshape(batch * seq, d_model)
    depth = norm_ffn1.shape[0]
    for l in range(depth):
        last = l == depth - 1
        h = _ffn_call(h, norm_ffn1[l], ffn1_gate[l], ffn1_up[l], ffn1_down[l])
        h = _mixer_call(h, batch, seq, norm_mix[l], w_in[l], w_out[l], ret_gn_gain[l],
                        pool_w[l], pool_scale[l])
        h = _ffn_call(h, norm_ffn2[l], ffn2_gate[l], ffn2_up[l], ffn2_down[l],
                      final_gain=norm_final if last else None)
    return h.reshape(batch, seq, d_model)
```

```python
import functools

import numpy as np
import jax
import jax.numpy as jnp
from jax import lax
from jax.experimental import pallas as pl
from jax.experimental.pallas import tpu as pltpu

D_MODEL = 1024
D_FF = 2816
CHUNK = 64
RET_HEADS = 4
RET_DK = 64
RET_DV = 128
RET_QKWIDTH = RET_HEADS * RET_DK
RET_VWIDTH = RET_HEADS * RET_DV
POOL_WINDOWS = (2, 4, 8, 16)
POOL_GC = 128
POOL_WIDTH = len(POOL_WINDOWS) * POOL_GC
POOL_HALO = 16
IN_WIDTH = 2 * RET_QKWIDTH + 2 * RET_VWIDTH + POOL_WIDTH
ROPE_BASE = 10000.0
RMS_EPS = 1e-6
GN_EPS = 1e-5

RET_BLOCK = 256
FFN_TILE = 1024
FFN_SUB = 512
FF_CHUNK = 256
MIX_TILE = 1024
WEIGHT_CHUNKS = 16
WEIGHT_RING = 4

LANES = 128

_F32 = jnp.float32
_BF16 = jnp.bfloat16
_MIB = 1024 * 1024


def _rms_scale(x):
    return lax.rsqrt(jnp.mean(x * x, axis=-1, keepdims=True) + RMS_EPS)


def _rms_norm(x, gain):
    return x * _rms_scale(x) * gain


def _dot(a, b):
    return jnp.dot(a, b, preferred_element_type=_F32)


def _chunk_copy(pairs, stage, sem, j):
    ring, chunk_rows = stage.shape[0], stage.shape[1]
    chunks_per_weight = pairs[0][0].shape[0] // chunk_rows
    src, c = pairs[j // chunks_per_weight][0], j % chunks_per_weight
    return pltpu.make_async_copy(src.at[pl.ds(c * chunk_rows, chunk_rows), :],
                                 stage.at[j % ring], sem.at[j % ring])


def _load_weights_bf16(groups):
    for pairs, stage, sem in groups:
        for j in range(WEIGHT_RING - 1):
            _chunk_copy(pairs, stage, sem, j).start()
    for pairs, stage, sem in groups:
        chunk_rows = stage.shape[1]
        n_chunks = WEIGHT_CHUNKS * len(pairs)
        for j in range(n_chunks):
            if j + WEIGHT_RING - 1 < n_chunks:
                _chunk_copy(pairs, stage, sem, j + WEIGHT_RING - 1).start()
            _chunk_copy(pairs, stage, sem, j).wait()
            dst, c = pairs[j // WEIGHT_CHUNKS][1], j % WEIGHT_CHUNKS
            dst[c * chunk_rows:(c + 1) * chunk_rows, :] = stage[j % WEIGHT_RING].astype(_BF16)


def _weight_scratch(shapes):
    homes = [pltpu.VMEM(s, _BF16) for s in shapes]
    rings = []
    for s in dict.fromkeys(shapes):
        rings += [pltpu.VMEM((WEIGHT_RING, s[0] // WEIGHT_CHUNKS, s[1]), _F32),
                  pltpu.SemaphoreType.DMA((WEIGHT_RING,))]
    return homes + rings


def _weight_scratch_bytes(shapes):
    homes = sum(2 * s[0] * s[1] for s in shapes)
    rings = sum(WEIGHT_RING * 4 * (s[0] // WEIGHT_CHUNKS) * s[1] for s in dict.fromkeys(shapes))
    return homes + rings


def _ffn_kernel(x_ref, g_ref, wg_hbm, wu_hbm, wd_hbm, *rest, final_norm):
    if final_norm:
        gf_ref, *rest = rest
    o_ref, wg_ref, wu_ref, wd_ref, up_stage, up_sem, down_stage, down_sem = rest

    @pl.when(pl.program_id(0) == 0)
    def _():
        _load_weights_bf16([([(wg_hbm, wg_ref), (wu_hbm, wu_ref)], up_stage, up_sem),
                            ([(wd_hbm, wd_ref)], down_stage, down_sem)])

    for sub in range(x_ref.shape[0] // FFN_SUB):
        rows = slice(sub * FFN_SUB, (sub + 1) * FFN_SUB)
        x = x_ref[rows, :]
        xg = (x * g_ref[...]).astype(_BF16)
        r = jnp.broadcast_to(_rms_scale(x), (FFN_SUB, FF_CHUNK))
        hidden = []
        for c in range(D_FF // FF_CHUNK):
            cols = slice(c * FF_CHUNK, (c + 1) * FF_CHUNK)
            gate = _dot(xg, wg_ref[:, cols]) * r
            up = _dot(xg, wu_ref[:, cols])
            hidden.append((gate * jax.nn.sigmoid(gate) * up).astype(_BF16))
        h = jnp.concatenate(hidden, axis=1)
        out = x + (0.5 * _rms_scale(x)) * _dot(h, wd_ref[...])
        if final_norm:
            out = _rms_norm(out, gf_ref[...])
        o_ref[rows, :] = out


def _resident(shape):
    zeros = (0,) * len(shape)
    return pl.BlockSpec(shape, lambda *_: zeros, pipeline_mode=pl.Buffered(1))


_IN_HBM = pl.BlockSpec(memory_space=pl.ANY)


def _ffn_call(x2d, gain, wg, wu, wd, final_gain=None):
    rows = x2d.shape[0]
    tm = FFN_TILE
    final_norm = final_gain is not None
    row_spec = pl.BlockSpec((tm, D_MODEL), lambda i: (i, 0))
    in_specs = [row_spec, _resident((1, D_MODEL)), _IN_HBM, _IN_HBM, _IN_HBM]
    args = [x2d, gain.reshape(1, D_MODEL), wg, wu, wd]
    if final_norm:
        in_specs.append(_resident((1, D_MODEL)))
        args.append(final_gain.reshape(1, D_MODEL))
    weight_shapes = [wg.shape, wu.shape, wd.shape]
    io_bytes = 4 * tm * D_MODEL * 4
    temp_bytes = tm * D_FF * 2 * 2 + tm * D_MODEL * (2 + 4 + 4) + 8 * FFN_SUB * FF_CHUNK * 4
    return pl.pallas_call(
        functools.partial(_ffn_kernel, final_norm=final_norm),
        out_shape=jax.ShapeDtypeStruct((rows, D_MODEL), _F32),
        grid=(rows // tm,),
        in_specs=in_specs,
        out_specs=row_spec,
        scratch_shapes=_weight_scratch(weight_shapes),
        compiler_params=pltpu.CompilerParams(
            dimension_semantics=("arbitrary",),
            vmem_limit_bytes=(_weight_scratch_bytes(weight_shapes) + io_bytes + temp_bytes +
                              4 * _MIB)),
        name="ffn_final" if final_norm else "ffn",
    )(*args)


def _mixer_tables(seq):
    heads = np.arange(RET_HEADS)
    gamma = 1.0 - 2.0 ** (-5.0 - heads)
    i = np.arange(RET_BLOCK)
    diff = i[:, None] - i[None, :]
    same = (i[:, None] // CHUNK) == (i[None, :] // CHUNK)
    earlier = (i[None, :] // CHUNK) < (i[:, None] // CHUNK)
    expo = np.where(same, np.abs(diff), diff)
    q_scale = RET_DK ** -0.5
    dmat = np.where((same | earlier)[None], gamma[:, None, None] ** expo[None], 0.0) * q_scale
    dq = np.repeat((gamma[None, :] ** (i[:, None] + 1.0)) * q_scale, RET_DV, axis=1)
    dk = np.repeat(gamma[None, :] ** (RET_BLOCK - 1.0 - i[:, None]), RET_DK, axis=1)
    dc = np.repeat(gamma ** float(RET_BLOCK), RET_DK)[:, None] * np.ones((1, RET_DV))
    half = RET_DK // 2
    freqs = ROPE_BASE ** (-np.arange(half) * 2.0 / RET_DK)
    ang = np.arange(seq)[:, None] * freqs[None, :]
    cos = np.tile(np.cos(ang), (1, LANES // half))
    sin = np.tile(np.concatenate([-np.sin(ang), np.sin(ang)], axis=1), (1, LANES // RET_DK))
    f32 = lambda a: jnp.asarray(a, dtype=_F32)
    return tuple(f32(a) for a in (cos, sin, dmat, dq, dk, dc))


def _rotary(t, cos, sin):
    lane = lax.broadcasted_iota(jnp.int32, (t.shape[0], LANES), 1)
    first_half = (lane & (RET_DK - 1)) < (RET_DK // 2)
    cols = []
    for c in range(t.shape[1] // LANES):
        tc = t[:, c * LANES:(c + 1) * LANES]
        partner = jnp.where(first_half, pltpu.roll(tc, LANES - RET_DK // 2, 1),
                            pltpu.roll(tc, RET_DK // 2, 1))
        cols.append(tc * cos + partner * sin)
    return jnp.concatenate(cols, axis=1)


def _mixer_kernel(x_ref, g_ref, win_hbm, wout_hbm, gain_ref, pw_ref, ps_ref,
                  cos_ref, sin_ref, dmat_ref, dq_ref, dk_ref, dc_ref,
                  o_ref, state_ref, u_ref, win_ref, wout_ref, in_stage, in_sem, out_stage, out_sem,
                  *, tm):
    s = pl.program_id(1)
    o1, o2, o3, o4 = RET_QKWIDTH, 2 * RET_QKWIDTH, 2 * RET_QKWIDTH + RET_VWIDTH, IN_WIDTH - POOL_WIDTH

    @pl.when((pl.program_id(0) == 0) & (s == 0))
    def _():
        _load_weights_bf16([([(win_hbm, win_ref)], in_stage, in_sem),
                            ([(wout_hbm, wout_ref)], out_stage, out_sem)])

    @pl.when(s == 0)
    def _():
        state_ref[...] = jnp.zeros_like(state_ref)
        u_ref[0:POOL_HALO, :] = jnp.zeros((POOL_HALO, POOL_WIDTH), _F32)

    @pl.when(s > 0)
    def _():
        u_ref[0:POOL_HALO, :] = u_ref[tm:tm + POOL_HALO, :]

    x = x_ref[...]
    n = (x * g_ref[...]).astype(_BF16)
    r_in = jnp.broadcast_to(_rms_scale(x), (tm, POOL_WIDTH))
    row0 = pl.multiple_of(s * tm, tm)

    u = _dot(n, win_ref[:, o4:]) * r_in
    u_ref[POOL_HALO:POOL_HALO + tm, :] = u
    pos = row0 + lax.broadcasted_iota(jnp.int32, (tm, 1), 0)
    pooled = []
    for gi, w in enumerate(POOL_WINDOWS):
        cs = slice(gi * POOL_GC, (gi + 1) * POOL_GC)
        acc = u_ref[:, cs]
        shift = 1
        while shift < w:
            acc = acc + pltpu.roll(acc, shift, 0)
            shift *= 2
        cnt = jnp.minimum(pos + 1, w).astype(_F32)
        pooled.append((acc[POOL_HALO:, :] / cnt - u[:, cs]).astype(_BF16))

    gate = _dot(n, win_ref[:, o3:o4]) * r_in
    gate = gate * jax.nn.sigmoid(gate)
    qkv = _dot(n, win_ref[:, :o3]) * jnp.concatenate([r_in, r_in], axis=1)
    pool = jnp.concatenate([_dot(pooled[gi], pw_ref[gi]) for gi in range(len(POOL_WINDOWS))],
                           axis=1)
    pool = (pool * ps_ref[...]).astype(_BF16)

    cos = cos_ref[pl.ds(row0, tm), :]
    sin = sin_ref[pl.ds(row0, tm), :]
    q = _rotary(qkv[:, :o1], cos, sin).astype(_BF16)
    k = _rotary(qkv[:, o1:o2], cos, sin)
    v = qkv[:, o2:o3].astype(_BF16)

    lane_head = lax.broadcasted_iota(jnp.int32, (RET_BLOCK, RET_QKWIDTH), 1) // RET_DK
    blocks = [slice(b * RET_BLOCK, (b + 1) * RET_BLOCK) for b in range(tm // RET_BLOCK)]
    states = [state_ref[...]]
    intra_blocks = []
    for rows in blocks:
        qb, kb, vb = q[rows], k[rows], v[rows]
        kb16 = kb.astype(_BF16)
        decayed = []
        for h in range(RET_HEADS):
            kh = jnp.where(lane_head == h, kb16, jnp.zeros_like(kb16))
            sc = lax.dot_general(qb, kh, (((1,), (1,)), ((), ())),
                                 preferred_element_type=_F32)
            decayed.append((sc * dmat_ref[h]).astype(_BF16))
        kd = (kb * dk_ref[...]).astype(_BF16)
        kv = lax.dot_general(kd, vb, (((0,), (0,)), ((), ())),
                             preferred_element_type=_F32)
        kv_diag = jnp.concatenate([kv[h * RET_DK:(h + 1) * RET_DK, h * RET_DV:(h + 1) * RET_DV]
                                   for h in range(RET_HEADS)], axis=0)
        states.append(states[-1] * dc_ref[...] + kv_diag)
        intra_blocks.append(jnp.concatenate(
            [_dot(decayed[h], vb[:, h * RET_DV:(h + 1) * RET_DV]) for h in range(RET_HEADS)],
            axis=1))
    state_ref[...] = states[-1]
    ret_blocks = []
    zero_block = jnp.zeros((RET_DK, RET_DV), _BF16)
    for rows, state, intra in zip(blocks, states, intra_blocks):
        state16 = state.astype(_BF16)
        expanded = jnp.concatenate(
            [jnp.concatenate([state16[h * RET_DK:(h + 1) * RET_DK] if j == h else zero_block
                              for j in range(RET_HEADS)], axis=1) for h in range(RET_HEADS)], axis=0)
        cross = _dot(q[rows], expanded) * dq_ref[...]
        ret_blocks.append(intra + cross)
    ret = ret_blocks[0] if len(ret_blocks) == 1 else jnp.concatenate(ret_blocks, axis=0)

    gn = []
    for h in range(RET_HEADS):
        o_h = ret[:, h * RET_DV:(h + 1) * RET_DV]
        mu = jnp.mean(o_h, axis=-1, keepdims=True)
        d = o_h - mu
        var = jnp.mean(d * d, axis=-1, keepdims=True)
        gn.append(d * lax.rsqrt(var + GN_EPS))
    r = gate * (jnp.concatenate(gn, axis=1) * gain_ref[...])

    mix = (_dot(r.astype(_BF16), wout_ref[0:RET_VWIDTH, :]) +
           _dot(pool, wout_ref[RET_VWIDTH:, :]))
    o_ref[...] = x + mix


def _mixer_call(x2d, batch, seq, gain, w_in, w_out, gn_gain, pool_w, pool_scale):
    tm = MIX_TILE
    steps = seq // tm
    row_spec = pl.BlockSpec((tm, D_MODEL), lambda b, s: (b * steps + s, 0))
    small = [gn_gain.reshape(1, RET_VWIDTH), pool_w.astype(_BF16),
             pool_scale.reshape(1, POOL_WIDTH), *_mixer_tables(seq)]
    in_specs = ([row_spec, _resident((1, D_MODEL)), _IN_HBM, _IN_HBM] +
                [_resident(a.shape) for a in small])
    weight_shapes = [w_in.shape, w_out.shape]
    resident_bytes = (sum(int(np.prod(a.shape)) * a.dtype.itemsize for a in small) +
                      _weight_scratch_bytes(weight_shapes))
    temp_bytes = tm * (IN_WIDTH * 4 * 2 + D_MODEL * 4 * 6) + 4 * RET_BLOCK * RET_BLOCK * 8
    return pl.pallas_call(
        functools.partial(_mixer_kernel, tm=tm),
        out_shape=jax.ShapeDtypeStruct(x2d.shape, _F32),
        grid=(batch, steps),
        in_specs=in_specs,
        out_specs=row_spec,
        scratch_shapes=[pltpu.VMEM((RET_QKWIDTH, RET_DV), _F32),
                        pltpu.VMEM((POOL_HALO + tm, POOL_WIDTH), _F32),
                        *_weight_scratch(weight_shapes)],
        compiler_params=pltpu.CompilerParams(
            dimension_semantics=("arbitrary", "arbitrary"),
            vmem_limit_bytes=resident_bytes + 4 * tm * D_MODEL * 4 + temp_bytes + 8 * _MIB),
        name="mixer",
    )(x2d, gain.reshape(1, D_MODEL), w_in, w_out, *small)


def kernel(x, norm_ffn1, ffn1_gate, ffn1_up, ffn1_down, norm_mix, w_in, ret_gn_gain, pool_w,
           pool_scale, w_out, norm_ffn2, ffn2_gate, ffn2_up, ffn2_down, norm_final):
    batch, seq, d_model = x.shape
    assert d_model == D_MODEL and seq % MIX_TILE == 0 and (batch * seq) % FFN_TILE == 0
    h = x.reshape(batch * seq, d_model)
    depth = norm_ffn1.shape[0]
    for l in range(depth):
        last = l == depth - 1
        h = _ffn_call(h, norm_ffn1[l], ffn1_gate[l], ffn1_up[l], ffn1_down[l])
        h = _mixer_call(h, batch, seq, norm_mix[l], w_in[l], w_out[l], ret_gn_gain[l],
                        pool_w[l], pool_scale[l])
        h = _ffn_call(h, norm_ffn2[l], ffn2_gate[l], ffn2_up[l], ffn2_down[l],
                      final_gain=norm_final if last else None)
    return h.reshape(batch, seq, d_model)
```
